```python
import jax
import jax.numpy as jnp
from jax import lax
import numpy as np

D_MODEL = 1024
BATCH = 4
SEQ = 4096
DEPTH = 1
DEC_BATCH = 128
DEC_SEQ = 4
PAST_LEN = 8192
PAGE_SIZE = 128

N_HEADS = 8
HEAD_DIM = 64
ATTN_WIDTH = N_HEADS * HEAD_DIM
CONV_WIDTH = D_MODEL - ATTN_WIDTH
MIX_WIDTH = ATTN_WIDTH + CONV_WIDTH
PROJ_WIDTH = 3 * ATTN_WIDTH + 2 * CONV_WIDTH
CONV_KERNEL = 31
MOBA_BLOCK = 256
MOBA_TOPK = 3
MOBA_Q_CHUNK = 32
N_EXPERTS = 256
TOP_K = 8
N_GROUPS = 8
TOPK_GROUPS = 4
GROUP_TOPN = 2
EXPERT_HIDDEN = 256
SHARED_HIDDEN = 256
ROUTED_SCALE = 2.5
MOE_BLOCK = 128
EPS = 1e-6

kernel_name = 'hymba_moba_conformer_moe_step'


def _rms(x, g):
    xf = x.astype(jnp.float32)
    y = xf * lax.rsqrt(jnp.mean(xf * xf, axis=-1, keepdims=True) + EPS)
    return (y * g.astype(jnp.float32)).astype(x.dtype)


def _layernorm(x, g, b):
    xf = x.astype(jnp.float32)
    xc = xf - jnp.mean(xf, axis=-1, keepdims=True)
    var = jnp.mean(xc * xc, axis=-1, keepdims=True)
    return (xc * lax.rsqrt(var + EPS) * g.astype(jnp.float32) + b.astype(jnp.float32)).astype(x.dtype)


def _adaln(c, w_ada, b_ada):
    mod = jax.nn.silu(c) @ w_ada + b_ada
    return jnp.split(mod[:, None, :], 6, axis=-1)


def _project(h, w_in, q_norm_g, k_norm_g):
    p = h @ w_in
    q, k, v, u = jnp.split(p, [ATTN_WIDTH, 2 * ATTN_WIDTH, 3 * ATTN_WIDTH], axis=-1)
    heads = h.shape[:-1] + (N_HEADS, HEAD_DIM)
    q = _rms(q.reshape(heads), q_norm_g)
    k = _rms(k.reshape(heads), k_norm_g)
    v = v.reshape(heads)
    u_a, u_b = jnp.split(u, 2, axis=-1)
    return q, k, v, u_a * jax.nn.sigmoid(u_b)


def _conv_branch(u_ext, conv_w, conv_b, ln_g, ln_b):
    y = lax.conv_general_dilated(u_ext, conv_w[:, None, :], window_strides=(1,), padding='VALID',
                                 dimension_numbers=('NWC', 'WIO', 'NWC'),
                                 feature_group_count=CONV_WIDTH) + conv_b
    return jax.nn.silu(_layernorm(y, ln_g, ln_b))


def _to_pages(t, page_size):
    n, s = t.shape[0], t.shape[1]
    return t.reshape(n, s // page_size, page_size, N_HEADS, HEAD_DIM).transpose(0, 1, 3, 2, 4)


def _moba_prompt(q, k, v):
    bsz, seq = q.shape[0], q.shape[1]
    q, k, v = q.transpose(0, 2, 1, 3), k.transpose(0, 2, 1, 3), v.transpose(0, 2, 1, 3)
    n_blk = -(-seq // MOBA_BLOCK)
    pad = n_blk * MOBA_BLOCK - seq
    kb = jnp.pad(k, ((0, 0), (0, 0), (0, pad), (0, 0))).reshape(bsz, N_HEADS, n_blk, MOBA_BLOCK, HEAD_DIM)
    vb = jnp.pad(v, ((0, 0), (0, 0), (0, pad), (0, 0))).reshape(bsz, N_HEADS, n_blk, MOBA_BLOCK, HEAD_DIM)
    n_sel = max(1, min(MOBA_TOPK, n_blk - 1))
    q_blk = jnp.arange(seq) // MOBA_BLOCK
    k_mean = jnp.mean(kb, axis=3, dtype=jnp.float32)
    gate = jnp.einsum('bhsd,bhnd->bhsn', q.astype(jnp.float32), k_mean)
    gate = jnp.where(jnp.arange(n_blk)[None, :] < q_blk[:, None], gate, -jnp.inf)
    _, sel = lax.top_k(gate, n_sel)
    sel_ok = sel < q_blk[:, None]
    n_qc = seq // MOBA_Q_CHUNK

    def chunked(t):
        return jnp.moveaxis(t.reshape((bsz, N_HEADS, n_qc, MOBA_Q_CHUNK) + t.shape[3:]), 2, 0)

    bi = jnp.arange(bsz)[:, None, None, None]
    hi = jnp.arange(N_HEADS)[None, :, None, None]
    scale = HEAD_DIM ** -0.5
    n_sel_keys = n_sel * MOBA_BLOCK

    def one_chunk(args):
        c, q_c, sel_c, ok_c = args
        start = c * MOBA_Q_CHUNK
        blk = start // MOBA_BLOCK
        q_pos = start + jnp.arange(MOBA_Q_CHUNK)
        k_pos = blk * MOBA_BLOCK + jnp.arange(MOBA_BLOCK)
        k_own = lax.dynamic_index_in_dim(kb, blk, axis=2, keepdims=False)
        v_own = lax.dynamic_index_in_dim(vb, blk, axis=2, keepdims=False)
        k_sel = kb[bi, hi, sel_c]
        v_sel = vb[bi, hi, sel_c]
        s_sel = jnp.einsum('bhqd,bhqnkd->bhqnk', q_c, k_sel).astype(jnp.float32) * scale
        s_sel = jnp.where(ok_c[..., None], s_sel, -jnp.inf).reshape(bsz, N_HEADS, MOBA_Q_CHUNK, n_sel_keys)
        s_own = jnp.einsum('bhqd,bhkd->bhqk', q_c, k_own).astype(jnp.float32) * scale
        s_own = jnp.where(k_pos[None, :] <= q_pos[:, None], s_own, -jnp.inf)
        p = jax.nn.softmax(jnp.concatenate([s_sel, s_own], axis=-1), axis=-1).astype(v.dtype)
        p_sel = p[..., :n_sel_keys].reshape(bsz, N_HEADS, MOBA_Q_CHUNK, n_sel, MOBA_BLOCK)
        return (jnp.einsum('bhqnk,bhqnkd->bhqd', p_sel, v_sel)
                + jnp.einsum('bhqk,bhkd->bhqd', p[..., n_sel_keys:], v_own))

    out = lax.map(one_chunk, (jnp.arange(n_qc), chunked(q), chunked(sel), chunked(sel_ok)))
    return out.transpose(1, 0, 3, 2, 4).reshape(bsz, seq, ATTN_WIDTH)


def _moba_sample(q, k, v, cache_k, cache_v, page_sum_k, page_table, layer):
    dbs, t_new = q.shape[0], q.shape[1]
    page_size = cache_k.shape[3]
    n_pages = page_table.shape[1]
    ppb = MOBA_BLOCK // page_size
    n_past_blk = (n_pages * page_size) // MOBA_BLOCK
    n_own_pages = n_pages - n_past_blk * ppb
    n_sel = max(1, min(MOBA_TOPK, n_past_blk))
    q, k, v = q.transpose(0, 2, 1, 3), k.transpose(0, 2, 1, 3), v.transpose(0, 2, 1, 3)
    if n_past_blk > 0:
        blk_pages = page_table[:, :n_past_blk * ppb].reshape(dbs, n_past_blk, ppb)
        k_mean = page_sum_k[blk_pages].sum(axis=2) / MOBA_BLOCK
        gate = jnp.einsum('bhtd,bnhd->bhtn', q.astype(jnp.float32), k_mean)
    else:
        gate = jnp.full((dbs, N_HEADS, t_new, 1), -jnp.inf, jnp.float32)
    _, sel = lax.top_k(gate, n_sel)
    sel_ok = sel < n_past_blk
    bi = jnp.arange(dbs)[:, None, None, None, None]
    hi = jnp.arange(N_HEADS)[None, :, None, None, None]
    logical = jnp.minimum(sel[..., None] * ppb + jnp.arange(ppb), n_pages - 1)
    phys = page_table[bi, logical]
    sel_shape = (dbs, N_HEADS, t_new, n_sel, MOBA_BLOCK, HEAD_DIM)
    k_sel = cache_k[layer, phys, hi].reshape(sel_shape)
    v_sel = cache_v[layer, phys, hi].reshape(sel_shape)
    own_phys = page_table[:, n_past_blk * ppb:]
    own_rows = n_own_pages * page_size
    k_own = cache_k[layer, own_phys].transpose(0, 2, 1, 3, 4).reshape(dbs, N_HEADS, own_rows, HEAD_DIM)
    v_own = cache_v[layer, own_phys].transpose(0, 2, 1, 3, 4).reshape(dbs, N_HEADS, own_rows, HEAD_DIM)
    scale = HEAD_DIM ** -0.5
    n_sel_keys = n_sel * MOBA_BLOCK
    s_sel = jnp.einsum('bhtd,bhtnkd->bhtnk', q, k_sel).astype(jnp.float32) * scale
    s_sel = jnp.where(sel_ok[..., None], s_sel, -jnp.inf).reshape(dbs, N_HEADS, t_new, n_sel_keys)
    s_own = jnp.einsum('bhtd,bhkd->bhtk', q, k_own).astype(jnp.float32) * scale
    s_new = jnp.einsum('bhtd,bhsd->bhts', q, k).astype(jnp.float32) * scale
    s_new = jnp.where(jnp.arange(t_new)[None, :] <= jnp.arange(t_new)[:, None], s_new, -jnp.inf)
    p = jax.nn.softmax(jnp.concatenate([s_sel, s_own, s_new], axis=-1), axis=-1).astype(v.dtype)
    p_sel = p[..., :n_sel_keys].reshape(dbs, N_HEADS, t_new, n_sel, MOBA_BLOCK)
    p_own = p[..., n_sel_keys:n_sel_keys + own_rows]
    p_new = p[..., n_sel_keys + own_rows:]
    out = (jnp.einsum('bhtnk,bhtnkd->bhtd', p_sel, v_sel)
           + jnp.einsum('bhtk,bhkd->bhtd', p_own, v_own)
           + jnp.einsum('bhts,bhsd->bhtd', p_new, v))
    return out.transpose(0, 2, 1, 3).reshape(dbs, t_new, ATTN_WIDTH)


def _swiglu(x, w_gate, w_up, w_down):
    return (jax.nn.silu(x @ w_gate) * (x @ w_up)) @ w_down


def _routed_experts(xt, top_e, gate_w, w_gate, w_up, w_down):
    n_tok = xt.shape[0]
    n_assign = n_tok * TOP_K
    e_flat = top_e.reshape(-1)
    tok_flat = jnp.repeat(jnp.arange(n_tok, dtype=jnp.int32), TOP_K)
    w_flat = gate_w.reshape(-1)
    order = jnp.argsort(e_flat)
    e_sorted = e_flat[order]
    counts = jnp.zeros((N_EXPERTS,), jnp.int32).at[e_flat].add(1)
    padded = (counts + MOE_BLOCK - 1) // MOE_BLOCK * MOE_BLOCK
    pad_end = jnp.cumsum(padded)
    pad_start = pad_end - padded
    grp_start = jnp.cumsum(counts) - counts
    dest = pad_start[e_sorted] + (jnp.arange(n_assign, dtype=jnp.int32) - grp_start[e_sorted])
    n_blocks = -(-(n_assign + N_EXPERTS * (MOE_BLOCK - 1)) // MOE_BLOCK)
    n_rows = n_blocks * MOE_BLOCK
    row_tok = jnp.full((n_rows,), n_tok, jnp.int32).at[dest].set(tok_flat[order])
    row_w = jnp.zeros((n_rows,), jnp.float32).at[dest].set(w_flat[order])
    blk_expert = jnp.minimum(jnp.searchsorted(pad_end, jnp.arange(n_blocks, dtype=jnp.int32) * MOE_BLOCK,
                                              side='right'), N_EXPERTS - 1)
    x_pad = jnp.concatenate([xt, jnp.zeros((1, D_MODEL), xt.dtype)], axis=0)
    x_rows = x_pad[row_tok].reshape(n_blocks, MOE_BLOCK, D_MODEL)

    def block_ffn(args):
        xb, e = args
        return _swiglu(xb, w_gate[e], w_up[e], w_down[e])

    y_rows = lax.map(block_ffn, (x_rows, blk_expert)).reshape(n_rows, D_MODEL)
    y = jax.ops.segment_sum(y_rows.astype(jnp.float32) * row_w[:, None], row_tok, num_segments=n_tok + 1)
    return y[:n_tok].astype(xt.dtype)


def _moe(h, w_router, router_bias, w_gate, w_up, w_down, ws_gate, ws_up, ws_down):
    xt = h.reshape(-1, D_MODEL)
    n_tok = xt.shape[0]
    score = jax.nn.sigmoid(xt.astype(jnp.float32) @ w_router.astype(jnp.float32))
    biased = score + router_bias.astype(jnp.float32)
    grp = biased.reshape(n_tok, N_GROUPS, N_EXPERTS // N_GROUPS)
    grp_score = lax.top_k(grp, GROUP_TOPN)[0].sum(axis=-1)
    _, top_g = lax.top_k(grp_score, TOPK_GROUPS)
    keep = jax.nn.one_hot(top_g, N_GROUPS, dtype=jnp.float32).sum(axis=1) > 0
    keep = jnp.repeat(keep, N_EXPERTS // N_GROUPS, axis=-1)
    _, top_e = lax.top_k(jnp.where(keep, biased, -jnp.inf), TOP_K)
    w = jnp.take_along_axis(score, top_e, axis=-1)
    w = w / jnp.sum(w, axis=-1, keepdims=True) * ROUTED_SCALE
    routed = _routed_experts(xt, top_e, w, w_gate, w_up, w_down)
    shared = _swiglu(xt, ws_gate, ws_up, ws_down)
    return (routed + shared).reshape(h.shape)


def _residual_block(x, c, mixer, norm1_g, norm2_g, w_ada, b_ada, moe_params):
    sh1, sc1, g1, sh2, sc2, g2 = _adaln(c, w_ada, b_ada)
    mix, new_state = mixer(_rms(x, norm1_g) * (1 + sc1) + sh1)
    x = x + g1 * mix
    x = x + g2 * _moe(_rms(x, norm2_g) * (1 + sc2) + sh2, *moe_params)
    return x, new_state


def setup_inputs(seed: int = 0) -> dict:
    key = jax.random.key(seed)
    ks = jax.random.split(key, 32)
    f32 = jnp.float32
    n_pages = PAST_LEN // PAGE_SIZE
    n_used = DEC_BATCH * n_pages
    n_pool = (n_used * 5 + 3) // 4

    def nrm(k, shape, scale):
        return jax.random.normal(k, shape, f32) * scale

    return {
        'x_prompt': nrm(ks[0], (BATCH, SEQ, D_MODEL), 1.0),
        'x_sample': nrm(ks[1], (DEC_BATCH, DEC_SEQ, D_MODEL), 1.0),
        'c_prompt': nrm(ks[2], (BATCH, D_MODEL), 1.0),
        'c_sample': nrm(ks[3], (DEC_BATCH, D_MODEL), 1.0),
        'cache_k': nrm(ks[4], (DEPTH, n_pool, N_HEADS, PAGE_SIZE, HEAD_DIM), 1.0),
        'cache_v': nrm(ks[5], (DEPTH, n_pool, N_HEADS, PAGE_SIZE, HEAD_DIM), 1.0),
        'state_conv': nrm(ks[6], (DEPTH, DEC_BATCH, CONV_KERNEL - 1, CONV_WIDTH), 0.5),
        'page_table': jax.random.permutation(ks[7], n_pool)[:n_used].reshape(DEC_BATCH, n_pages).astype(jnp.int32),
        'norm1_g': 1.0 + nrm(ks[8], (DEPTH, D_MODEL), 0.02),
        'norm2_g': 1.0 + nrm(ks[9], (DEPTH, D_MODEL), 0.02),
        'w_ada': nrm(ks[10], (DEPTH, D_MODEL, 6 * D_MODEL), 0.5 * D_MODEL ** -0.5),
        'b_ada': nrm(ks[11], (DEPTH, 6 * D_MODEL), 0.02),
        'w_in': nrm(ks[12], (DEPTH, D_MODEL, PROJ_WIDTH), D_MODEL ** -0.5),
        'q_norm_g': 1.0 + nrm(ks[13], (DEPTH, HEAD_DIM), 0.02),
        'k_norm_g': 1.0 + nrm(ks[14], (DEPTH, HEAD_DIM), 0.02),
        'conv_w': nrm(ks[15], (DEPTH, CONV_KERNEL, CONV_WIDTH), CONV_KERNEL ** -0.5),
        'conv_b': nrm(ks[16], (DEPTH, CONV_WIDTH), 0.02),
        'conv_ln_g': 1.0 + nrm(ks[17], (DEPTH, CONV_WIDTH), 0.02),
        'conv_ln_b': nrm(ks[18], (DEPTH, CONV_WIDTH), 0.02),
        'w_out': nrm(ks[19], (DEPTH, MIX_WIDTH, D_MODEL), MIX_WIDTH ** -0.5),
        'w_router': nrm(ks[20], (DEPTH, D_MODEL, N_EXPERTS), D_MODEL ** -0.5),
        'router_bias': nrm(ks[21], (DEPTH, N_EXPERTS), 0.01),
        'w_gate': nrm(ks[22], (DEPTH, N_EXPERTS, D_MODEL, EXPERT_HIDDEN), D_MODEL ** -0.5),
        'w_up': nrm(ks[23], (DEPTH, N_EXPERTS, D_MODEL, EXPERT_HIDDEN), D_MODEL ** -0.5),
        'w_down': nrm(ks[24], (DEPTH, N_EXPERTS, EXPERT_HIDDEN, D_MODEL), EXPERT_HIDDEN ** -0.5),
        'w_shared_gate': nrm(ks[25], (DEPTH, D_MODEL, SHARED_HIDDEN), D_MODEL ** -0.5),
        'w_shared_up': nrm(ks[26], (DEPTH, D_MODEL, SHARED_HIDDEN), D_MODEL ** -0.5),
        'w_shared_down': nrm(ks[27], (DEPTH, SHARED_HIDDEN, D_MODEL), SHARED_HIDDEN ** -0.5),
    }


def reference(x_prompt, x_sample, c_prompt, c_sample, cache_k, cache_v, state_conv, page_table,
              norm1_g, norm2_g, w_ada, b_ada, w_in, q_norm_g, k_norm_g, conv_w, conv_b, conv_ln_g, conv_ln_b,
              w_out, w_router, router_bias, w_gate, w_up, w_down, w_shared_gate, w_shared_up, w_shared_down):
    page_sum_k = jnp.sum(cache_k, axis=3, dtype=jnp.float32)
    y_p, y_s = x_prompt, x_sample
    k_p_all, v_p_all, c_p_all, k_s_all, v_s_all, c_s_all = [], [], [], [], [], []
    for layer in range(DEPTH):
        w_in_l, qg, kg, w_out_l = w_in[layer], q_norm_g[layer], k_norm_g[layer], w_out[layer]
        conv_params = (conv_w[layer], conv_b[layer], conv_ln_g[layer], conv_ln_b[layer])
        moe_params = (w_router[layer], router_bias[layer], w_gate[layer], w_up[layer], w_down[layer],
                      w_shared_gate[layer], w_shared_up[layer], w_shared_down[layer])
        block_params = (norm1_g[layer], norm2_g[layer], w_ada[layer], b_ada[layer], moe_params)
        conv_state_l = state_conv[layer]
        page_sum_l = page_sum_k[layer]

        def prompt_mixer(h):
            q, k, v, u = _project(h, w_in_l, qg, kg)
            attn = _moba_prompt(q, k, v)
            u_ext = jnp.pad(u, ((0, 0), (CONV_KERNEL - 1, 0), (0, 0)))
            conv = _conv_branch(u_ext, *conv_params)
            mix = jnp.concatenate([attn, conv], axis=-1) @ w_out_l
            return mix, (_to_pages(k, PAGE_SIZE), _to_pages(v, PAGE_SIZE), u_ext[:, -(CONV_KERNEL - 1):])

        def sample_mixer(h):
            q, k, v, u = _project(h, w_in_l, qg, kg)
            attn = _moba_sample(q, k, v, cache_k, cache_v, page_sum_l, page_table, layer)
            u_ext = jnp.concatenate([conv_state_l, u], axis=1)
            conv = _conv_branch(u_ext, *conv_params)
            mix = jnp.concatenate([attn, conv], axis=-1) @ w_out_l
            return mix, (k.transpose(0, 2, 1, 3), v.transpose(0, 2, 1, 3), u_ext[:, -(CONV_KERNEL - 1):])

        y_p, (k_p, v_p, c_p) = _residual_block(y_p, c_prompt, prompt_mixer, *block_params)
        y_s, (k_s, v_s, c_s) = _residual_block(y_s, c_sample, sample_mixer, *block_params)
        k_p_all.append(k_p)
        v_p_all.append(v_p)
        c_p_all.append(c_p)
        k_s_all.append(k_s)
        v_s_all.append(v_s)
        c_s_all.append(c_s)
    return (y_p, y_s, jnp.stack(k_p_all), jnp.stack(v_p_all), jnp.stack(c_p_all),
            jnp.stack(k_s_all), jnp.stack(v_s_all), jnp.stack(c_s_all))
```

```python
import functools

import jax
import jax.numpy as jnp
from jax import lax
from jax.experimental import pallas as pl
from jax.experimental.pallas import tpu as pltpu

F32 = jnp.float32
BF16 = jnp.bfloat16
I32 = jnp.int32

D_MODEL = 1024
N_HEADS = 8
HEAD_DIM = 64
ATTN_WIDTH = N_HEADS * HEAD_DIM
CONV_WIDTH = D_MODEL - ATTN_WIDTH
PROJ_WIDTH = 3 * ATTN_WIDTH + 2 * CONV_WIDTH
CONV_KERNEL = 31
MOBA_BLOCK = 256
MOBA_TOPK = 3
PAGE_SIZE = 128
N_EXPERTS = 256
TOP_K = 8
N_GROUPS = 8
TOPK_GROUPS = 4
GROUP_SIZE = N_EXPERTS // N_GROUPS
EXPERT_HIDDEN = 256
ROUTED_SCALE = 2.5
EPS = 1e-6

TOK_TILE = 256
MOE_BLOCK = 128
COMBINE_TILE = 64
PAGES_PER_SUM_STEP = 128
VMEM_LIMIT = 48 * 1024 * 1024
NEG_INF = float("-inf")


def _split(a):
    hi = a.astype(BF16)
    lo = (a - hi.astype(F32)).astype(BF16)
    return hi, lo


def _dot(a, b):
    return jnp.dot(a, b, preferred_element_type=F32)


def _dot_nt(a, b):
    return lax.dot_general(a, b, (((1,), (1,)), ((), ())), preferred_element_type=F32)


def _params(sem, vmem=VMEM_LIMIT):
    return pltpu.CompilerParams(dimension_semantics=sem, vmem_limit_bytes=vmem)


def _ada_kernel(c_ref, w_ref, b_ref, o_ref):
    c = c_ref[...]
    s_hi, s_lo = _split(c * jax.nn.sigmoid(c))
    w_hi, w_lo = _split(w_ref[...])
    o_ref[...] = _dot(s_hi, w_hi) + _dot(s_hi, w_lo) + _dot(s_lo, w_hi) + b_ref[...]


def _adaln(c_all, w_ada, b_ada):
    n = c_all.shape[0]
    n_chunks = w_ada.shape[1] // D_MODEL
    return pl.pallas_call(
        _ada_kernel,
        out_shape=jax.ShapeDtypeStruct((n, w_ada.shape[1]), F32),
        grid=(n_chunks,),
        in_specs=[pl.BlockSpec((n, D_MODEL), lambda j: (0, 0)),
                  pl.BlockSpec((D_MODEL, D_MODEL), lambda j: (0, j)),
                  pl.BlockSpec((1, D_MODEL), lambda j: (0, j))],
        out_specs=pl.BlockSpec((n, D_MODEL), lambda j: (0, j)),
        compiler_params=_params(("arbitrary",)),
        name="adaln",
    )(c_all, w_ada, b_ada)


def _mod_spec(per_token, chunk, rows, rows_per_seq):
    if per_token:
        return pl.BlockSpec((rows, D_MODEL), lambda i, *_: (i, chunk))
    steps = rows_per_seq // rows
    return pl.BlockSpec((None, 1, D_MODEL), lambda i, *_: (i // steps, 0, chunk))


def _proj_kernel(x_ref, sh_ref, sc_ref, n1_ref, w_ref, qg_ref, kg_ref, hm_ref, *outs, prompt):
    x = x_ref[...]
    h = x * lax.rsqrt(jnp.mean(x * x, axis=-1, keepdims=True) + EPS) * n1_ref[...]
    h = h * (1.0 + sc_ref[...]) + sh_ref[...]
    p = _dot(h.astype(BF16), w_ref[...])
    hm = hm_ref[...]

    def head_norm(t, g):
        hi, lo = _split(t * t)
        ms = _dot(hi, hm) + _dot(lo, hm)
        return t * lax.rsqrt(ms + EPS) * g

    q = head_norm(p[:, :ATTN_WIDTH], qg_ref[...])
    k = head_norm(p[:, ATTN_WIDTH:2 * ATTN_WIDTH], kg_ref[...])
    v = p[:, 2 * ATTN_WIDTH:3 * ATTN_WIDTH]
    u_a = p[:, 3 * ATTN_WIDTH:3 * ATTN_WIDTH + CONV_WIDTH]
    u_b = p[:, 3 * ATTN_WIDTH + CONV_WIDTH:]
    u = u_a * jax.nn.sigmoid(u_b)
    if prompt:
        qb_ref, kb_ref, vb_ref, kpg_ref, vpg_ref, km_ref, u_ref = outs
        qb_ref[...] = (q * (HEAD_DIM ** -0.5)).astype(BF16)
        kb_ref[...] = k.astype(BF16)
        vb_ref[...] = v.astype(BF16)
        for pg in range(TOK_TILE // PAGE_SIZE):
            rows = slice(pg * PAGE_SIZE, (pg + 1) * PAGE_SIZE)
            for hd in range(N_HEADS):
                cols = slice(hd * HEAD_DIM, (hd + 1) * HEAD_DIM)
                kpg_ref[pg, hd] = k[rows, cols]
                vpg_ref[pg, hd] = v[rows, cols]
        km_ref[...] = jnp.mean(k, axis=0, keepdims=True)
        u_ref[...] = u
    else:
        q_ref, k_ref, v_ref, u_ref = outs
        q_ref[...] = q
        k_ref[...] = k
        v_ref[...] = v
        u_ref[...] = u


def _project(x2d, mod, per_token, rows_per_seq, norm_g, w_in_bf, qg, kg, head_mean, prompt):
    n_tok = x2d.shape[0]
    n_steps = n_tok // TOK_TILE
    tok = lambda w, dt: jax.ShapeDtypeStruct((n_tok, w), dt)
    tok_spec = lambda w: pl.BlockSpec((TOK_TILE, w), lambda i: (i, 0))
    const = lambda r, c: pl.BlockSpec((r, c), lambda i: (0, 0))
    if prompt:
        n_pages = n_tok // PAGE_SIZE
        ppt = TOK_TILE // PAGE_SIZE
        pg_shape = jax.ShapeDtypeStruct((n_pages, N_HEADS, PAGE_SIZE, HEAD_DIM), F32)
        pg_spec = pl.BlockSpec((ppt, N_HEADS, PAGE_SIZE, HEAD_DIM), lambda i: (i, 0, 0, 0))
        out_shape = (tok(ATTN_WIDTH, BF16), tok(ATTN_WIDTH, BF16), tok(ATTN_WIDTH, BF16), pg_shape, pg_shape,
                     jax.ShapeDtypeStruct((n_steps, 1, ATTN_WIDTH), F32), tok(CONV_WIDTH, F32))
        out_specs = (tok_spec(ATTN_WIDTH), tok_spec(ATTN_WIDTH), tok_spec(ATTN_WIDTH), pg_spec, pg_spec,
                     pl.BlockSpec((None, 1, ATTN_WIDTH), lambda i: (i, 0, 0)), tok_spec(CONV_WIDTH))
    else:
        out_shape = (tok(ATTN_WIDTH, F32), tok(ATTN_WIDTH, F32), tok(ATTN_WIDTH, F32), tok(CONV_WIDTH, F32))
        out_specs = (tok_spec(ATTN_WIDTH),) * 3 + (tok_spec(CONV_WIDTH),)
    return pl.pallas_call(
        functools.partial(_proj_kernel, prompt=prompt),
        out_shape=out_shape,
        grid=(n_steps,),
        in_specs=[tok_spec(D_MODEL),
                  _mod_spec(per_token, 0, TOK_TILE, rows_per_seq),
                  _mod_spec(per_token, 1, TOK_TILE, rows_per_seq),
                  const(1, D_MODEL), const(D_MODEL, PROJ_WIDTH),
                  const(1, ATTN_WIDTH), const(1, ATTN_WIDTH), const(ATTN_WIDTH, ATTN_WIDTH)],
        out_specs=out_specs,
        compiler_params=_params(("arbitrary",)),
        name="proj_prompt" if prompt else "proj_sample",
    )(x2d, mod, mod, norm_g, w_in_bf, qg, kg, head_mean)


def _attn_prompt_kernel(q_ref, k_ref, v_ref, km_ref, o_ref):
    i = pl.program_id(2)
    n_blk = km_ref.shape[1]
    col = lax.broadcasted_iota(I32, (MOBA_BLOCK, n_blk), 1)
    row_q = lax.broadcasted_iota(I32, (MOBA_BLOCK, MOBA_BLOCK), 0)
    col_k = lax.broadcasted_iota(I32, (MOBA_BLOCK, MOBA_BLOCK), 1)
    outs = []
    for hd in range(2):
        lanes = slice(hd * HEAD_DIM, (hd + 1) * HEAD_DIM)
        q = q_ref[0, :, lanes]
        km_hi, km_lo = _split(km_ref[0, :, lanes])
        gate = _dot_nt(q, km_hi) + _dot_nt(q, km_lo)
        g = jnp.where(col < i, gate, NEG_INF)
        sel = jnp.zeros(g.shape, F32)
        for _ in range(MOBA_TOPK):
            m = jnp.max(g, axis=1, keepdims=True)
            idx = jnp.min(jnp.where(g == m, col, n_blk), axis=1, keepdims=True)
            pick = col == idx
            sel = jnp.where(pick, 1.0, sel)
            g = jnp.where(pick, NEG_INF, g)
        neg = jnp.where(col < i, jnp.where(sel > 0.0, 0.0, NEG_INF), NEG_INF)

        start = pl.multiple_of(i * MOBA_BLOCK, MOBA_BLOCK)
        s = _dot_nt(q, k_ref[0, pl.ds(start, MOBA_BLOCK), lanes])
        s = jnp.where(col_k <= row_q, s, NEG_INF)
        m0 = jnp.max(s, axis=1, keepdims=True)
        p = jnp.exp(s - m0)
        l0 = jnp.sum(p, axis=1, keepdims=True)
        acc0 = _dot(p.astype(BF16), v_ref[0, pl.ds(start, MOBA_BLOCK), lanes])

        def body(j, carry):
            m, l, acc = carry
            off = pl.multiple_of(j * MOBA_BLOCK, MOBA_BLOCK)
            bias = jnp.max(jnp.where(col == j, neg, NEG_INF), axis=1, keepdims=True)
            s = _dot_nt(q, k_ref[0, pl.ds(off, MOBA_BLOCK), lanes]) + bias
            m_new = jnp.maximum(m, jnp.max(s, axis=1, keepdims=True))
            a = jnp.exp(m - m_new)
            p = jnp.exp(s - m_new)
            l = a * l + jnp.sum(p, axis=1, keepdims=True)
            acc = a * acc + _dot(p.astype(BF16), v_ref[0, pl.ds(off, MOBA_BLOCK), lanes])
            return m_new, l, acc

        _, l, acc = lax.fori_loop(0, i, body, (m0, l0, acc0))
        outs.append(acc / l)
    o_ref[0] = jnp.concatenate(outs, axis=1).astype(BF16)


def _attn_prompt(q, k, v, kmean):
    bsz, seq, _ = q.shape
    n_blk = seq // MOBA_BLOCK
    pair = 2 * HEAD_DIM
    return pl.pallas_call(
        _attn_prompt_kernel,
        out_shape=jax.ShapeDtypeStruct((bsz, seq, ATTN_WIDTH), BF16),
        grid=(bsz, N_HEADS // 2, n_blk),
        in_specs=[pl.BlockSpec((1, MOBA_BLOCK, pair), lambda b, h, i: (b, i, h)),
                  pl.BlockSpec((1, seq, pair), lambda b, h, i: (b, 0, h)),
                  pl.BlockSpec((1, seq, pair), lambda b, h, i: (b, 0, h)),
                  pl.BlockSpec((1, n_blk, pair), lambda b, h, i: (b, 0, h))],
        out_specs=pl.BlockSpec((1, MOBA_BLOCK, pair), lambda b, h, i: (b, i, h)),
        compiler_params=_params(("arbitrary", "arbitrary", "arbitrary")),
        name="attn_prompt",
    )(q, k, v, kmean)


def _ln_swish(y, g, b):
    mean = jnp.mean(y, axis=-1, keepdims=True)
    yc = y - mean
    var = jnp.mean(yc * yc, axis=-1, keepdims=True)
    z = yc * lax.rsqrt(var + EPS) * g + b
    return z * jax.nn.sigmoid(z)


CONV_ROWS = 32
CONV_HALO = 32


def _conv_prompt_kernel(cur_ref, prev_ref, cw_ref, cb_ref, g_ref, b_ref, o_ref, ext_ref):
    i = pl.program_id(1)
    ext_ref[0:CONV_HALO, :] = jnp.where(i > 0, prev_ref[0], 0.0)
    ext_ref[CONV_HALO:, :] = cur_ref[0]
    lead = CONV_HALO - (CONV_KERNEL - 1)
    for c in range(TOK_TILE // CONV_ROWS):
        r0 = c * CONV_ROWS
        acc = jnp.zeros((CONV_ROWS, CONV_WIDTH), F32)
        for j in range(CONV_KERNEL):
            acc = acc + cw_ref[j:j + 1, :] * ext_ref[r0 + lead + j:r0 + lead + j + CONV_ROWS, :]
        y = _ln_swish(acc + cb_ref[...], g_ref[...], b_ref[...])
        o_ref[0, r0:r0 + CONV_ROWS, :] = y.astype(BF16)


def _conv_prompt(u, cw, cb, ln_g, ln_b):
    bsz, seq, _ = u.shape
    halo_per_tile = TOK_TILE // CONV_HALO
    const = lambda r: pl.BlockSpec((r, CONV_WIDTH), lambda b, i: (0, 0))
    return pl.pallas_call(
        _conv_prompt_kernel,
        out_shape=jax.ShapeDtypeStruct((bsz, seq, CONV_WIDTH), BF16),
        grid=(bsz, seq // TOK_TILE),
        in_specs=[pl.BlockSpec((1, TOK_TILE, CONV_WIDTH), lambda b, i: (b, i, 0)),
                  pl.BlockSpec((1, CONV_HALO, CONV_WIDTH),
                               lambda b, i: (b, jnp.maximum(i * halo_per_tile - 1, 0), 0)),
                  const(CONV_KERNEL), const(1), const(1), const(1)],
        out_specs=pl.BlockSpec((1, TOK_TILE, CONV_WIDTH), lambda b, i: (b, i, 0)),
        scratch_shapes=[pltpu.VMEM((CONV_HALO + TOK_TILE, CONV_WIDTH), F32)],
        compiler_params=_params(("arbitrary", "arbitrary")),
        name="conv_prompt",
    )(u, u, cw, cb, ln_g, ln_b)


def _conv_sample_kernel(ext_ref, cw_ref, cb_ref, g_ref, b_ref, o_ref):
    for t in range(o_ref.shape[0]):
        acc = jnp.zeros(o_ref.shape[1:], F32)
        for j in range(CONV_KERNEL):
            acc = acc + cw_ref[j:j + 1, :] * ext_ref[t + j]
        o_ref[t] = _ln_swish(acc + cb_ref[...], g_ref[...], b_ref[...]).astype(BF16)


def _conv_sample(ext_tm, cw, cb, ln_g, ln_b, t_new):
    n_ext, n_seq, _ = ext_tm.shape
    seqs = 32
    const = lambda r: pl.BlockSpec((r, CONV_WIDTH), lambda i: (0, 0))
    return pl.pallas_call(
        _conv_sample_kernel,
        out_shape=jax.ShapeDtypeStruct((t_new, n_seq, CONV_WIDTH), BF16),
        grid=(n_seq // seqs,),
        in_specs=[pl.BlockSpec((n_ext, seqs, CONV_WIDTH), lambda i: (0, i, 0)),
                  const(CONV_KERNEL), const(1), const(1), const(1)],
        out_specs=pl.BlockSpec((t_new, seqs, CONV_WIDTH), lambda i: (0, i, 0)),
        compiler_params=_params(("arbitrary",)),
        name="conv_sample",
    )(ext_tm, cw, cb, ln_g, ln_b)


def _page_sum_kernel(x_ref, o_ref):
    o_ref[...] = jnp.sum(x_ref[...], axis=1)


def _page_sums(cache_rows):
    n = cache_rows.shape[0]
    return pl.pallas_call(
        _page_sum_kernel,
        out_shape=jax.ShapeDtypeStruct((n, HEAD_DIM), F32),
        grid=(n // PAGES_PER_SUM_STEP,),
        in_specs=[pl.BlockSpec((PAGES_PER_SUM_STEP, PAGE_SIZE, HEAD_DIM), lambda i: (i, 0, 0))],
        out_specs=pl.BlockSpec((PAGES_PER_SUM_STEP, HEAD_DIM), lambda i: (i, 0)),
        compiler_params=_params(("arbitrary",)),
        name="page_sums",
    )(cache_rows)


def _gate_sample_kernel(pt_ref, ps_ref, q_ref, o_ref, buf_a, buf_b, gate_ref, sem, *, n_pages, t_new):
    b = pl.program_id(0)
    n_blk = n_pages // 2

    def copies():
        out = []
        for n in range(n_blk):
            pa = pt_ref[b * n_pages + 2 * n]
            pb = pt_ref[b * n_pages + 2 * n + 1]
            out.append(pltpu.make_async_copy(ps_ref.at[pl.ds(pa, 1), :], buf_a.at[pl.ds(n, 1), :], sem.at[0]))
            out.append(pltpu.make_async_copy(ps_ref.at[pl.ds(pb, 1), :], buf_b.at[pl.ds(n, 1), :], sem.at[1]))
        return out

    cps = copies()
    for c in cps:
        c.start()
    for c in cps:
        c.wait()
    km = (buf_a[...] + buf_b[...]) * (1.0 / MOBA_BLOCK)
    q = q_ref[0]
    for hd in range(N_HEADS):
        lanes = slice(hd * HEAD_DIM, (hd + 1) * HEAD_DIM)
        q_hi, q_lo = _split(q[:, lanes])
        k_hi, k_lo = _split(km[:, lanes])
        gate_ref[hd * t_new:(hd + 1) * t_new, :] = (_dot_nt(q_hi, k_hi) + _dot_nt(q_hi, k_lo)
                                                    + _dot_nt(q_lo, k_hi))
    g = gate_ref[...]
    col = lax.broadcasted_iota(I32, g.shape, 1)
    lane = lax.broadcasted_iota(I32, o_ref.shape[1:], 1)
    out = jnp.zeros(o_ref.shape[1:], I32)
    for r in range(MOBA_TOPK):
        m = jnp.max(g, axis=1, keepdims=True)
        idx = jnp.min(jnp.where(g == m, col, n_blk), axis=1, keepdims=True)
        g = jnp.where(col == idx, NEG_INF, g)
        out = jnp.where(lane == r, idx, out)
    o_ref[0] = out


def _gate_sample(page_table, page_sum2d, q_s, t_new):
    dbs, n_pages = page_table.shape
    n_blk = n_pages // 2
    rows = N_HEADS * t_new
    grid_spec = pltpu.PrefetchScalarGridSpec(
        num_scalar_prefetch=1,
        grid=(dbs,),
        in_specs=[pl.BlockSpec(memory_space=pl.ANY),
                  pl.BlockSpec((1, t_new, ATTN_WIDTH), lambda b, pt: (b, 0, 0))],
        out_specs=pl.BlockSpec((1, rows, 128), lambda b, pt: (b, 0, 0)),
        scratch_shapes=[pltpu.VMEM((n_blk, ATTN_WIDTH), F32), pltpu.VMEM((n_blk, ATTN_WIDTH), F32),
                        pltpu.VMEM((rows, n_blk), F32), pltpu.SemaphoreType.DMA((2,))])
    return pl.pallas_call(
        functools.partial(_gate_sample_kernel, n_pages=n_pages, t_new=t_new),
        out_shape=jax.ShapeDtypeStruct((dbs, rows, 128), I32),
        grid_spec=grid_spec,
        compiler_params=_params(("arbitrary",)),
        name="gate_sample",
    )(page_table.reshape(-1), page_sum2d, q_s)


def _attn_sample_kernel(idx_ref, ck_ref, cv_ref, q_ref, kn_ref, vn_ref, o_ref, kbuf, vbuf, sem,
                        *, n_fetch, t_new):
    b = pl.program_id(0)
    hd = pl.program_id(1)
    lin = b * pl.num_programs(1) + hd
    n_lin = pl.num_programs(0) * pl.num_programs(1)
    slot = lin % 2
    per_tok = n_fetch // t_new

    def copies(step, slot_):
        out = []
        for j in range(n_fetch):
            r = idx_ref[step * n_fetch + j]
            out.append(pltpu.make_async_copy(ck_ref.at[r], kbuf.at[slot_, j], sem.at[slot_, 0]))
            out.append(pltpu.make_async_copy(cv_ref.at[r], vbuf.at[slot_, j], sem.at[slot_, 1]))
        return out

    @pl.when(lin == 0)
    def _():
        for c in copies(lin, slot):
            c.start()

    @pl.when(lin + 1 < n_lin)
    def _():
        for c in copies(lin + 1, 1 - slot):
            c.start()

    for c in copies(lin, slot):
        c.wait()

    q = q_ref[0, 0] * (HEAD_DIM ** -0.5)
    qb = q.astype(BF16)
    kn = kn_ref[0, 0]
    vn = vn_ref[0, 0]
    row = lax.broadcasted_iota(I32, (t_new, 1), 0)
    s_new = [jnp.where(row >= s, jnp.sum(q * kn[s:s + 1, :], axis=1, keepdims=True), NEG_INF)
             for s in range(t_new)]
    m_new = functools.reduce(jnp.maximum, s_new)
    out = jnp.zeros((t_new, HEAD_DIM), F32)
    for t in range(t_new):
        kt = kbuf[slot, pl.ds(t * per_tok, per_tok)].reshape(per_tok * PAGE_SIZE, HEAD_DIM).astype(BF16)
        vt = vbuf[slot, pl.ds(t * per_tok, per_tok)].reshape(per_tok * PAGE_SIZE, HEAD_DIM).astype(BF16)
        s = _dot_nt(qb, kt)
        m = jnp.maximum(jnp.max(s, axis=1, keepdims=True), m_new)
        p = jnp.exp(s - m)
        l = jnp.sum(p, axis=1, keepdims=True)
        o = _dot(p.astype(BF16), vt)
        for s_i in range(t_new):
            pn = jnp.exp(s_new[s_i] - m)
            l = l + pn
            o = o + pn * vn[s_i:s_i + 1, :]
        out = jnp.where(row == t, o / l, out)
    o_ref[0, 0] = out


def _attn_sample(row_idx, cache_k_rows, cache_v_rows, q4, k4, v4, n_fetch):
    dbs, n_heads, t_new, _ = q4.shape
    blk = pl.BlockSpec((1, 1, t_new, HEAD_DIM), lambda b, h, idx: (b, h, 0, 0))
    grid_spec = pltpu.PrefetchScalarGridSpec(
        num_scalar_prefetch=1,
        grid=(dbs, n_heads),
        in_specs=[pl.BlockSpec(memory_space=pl.ANY), pl.BlockSpec(memory_space=pl.ANY), blk, blk, blk],
        out_specs=blk,
        scratch_shapes=[pltpu.VMEM((2, n_fetch, PAGE_SIZE, HEAD_DIM), F32),
                        pltpu.VMEM((2, n_fetch, PAGE_SIZE, HEAD_DIM), F32),
                        pltpu.SemaphoreType.DMA((2, 2))])
    return pl.pallas_call(
        functools.partial(_attn_sample_kernel, n_fetch=n_fetch, t_new=t_new),
        out_shape=jax.ShapeDtypeStruct(q4.shape, F32),
        grid_spec=grid_spec,
        compiler_params=_params(("arbitrary", "arbitrary")),
        name="attn_sample",
    )(row_idx, cache_k_rows, cache_v_rows, q4, k4, v4)


def _out_kernel(attn_ref, conv_ref, x_ref, g1_ref, sh2_ref, sc2_ref, g2_ref, n2_ref, wo_ref, wr_hi_ref,
                wr_lo_ref, wgu_ref, wd_ref, *refs):
    base_ref, h_ref, logit_ref = refs[-3:]
    a = jnp.concatenate([attn_ref[...], conv_ref[...]], axis=1)
    x1 = x_ref[...] + g1_ref[...] * _dot(a, wo_ref[...])
    h = x1 * lax.rsqrt(jnp.mean(x1 * x1, axis=-1, keepdims=True) + EPS) * n2_ref[...]
    h = h * (1.0 + sc2_ref[...]) + sh2_ref[...]
    h_hi, h_lo = _split(h)
    logit_ref[...] = _dot(h_hi, wr_hi_ref[...]) + _dot(h_hi, wr_lo_ref[...]) + _dot(h_lo, wr_hi_ref[...])
    gu = _dot(h_hi, wgu_ref[...])
    gate = gu[:, :EXPERT_HIDDEN]
    act = gate * jax.nn.sigmoid(gate) * gu[:, EXPERT_HIDDEN:]
    base_ref[...] = x1 + g2_ref[...] * _dot(act.astype(BF16), wd_ref[...])
    h_ref[...] = h


def _out_proj(attn, conv, x2d, mod, per_token, rows_per_seq, n2g, wo, wr_hi, wr_lo, wgu, wd, n_all, tile_off,
              carried):
    n_tok = x2d.shape[0]
    tok_spec = lambda w: pl.BlockSpec((TOK_TILE, w), lambda i: (i, 0))
    out_spec = lambda w: pl.BlockSpec((TOK_TILE, w), lambda i: (i + tile_off, 0))
    const = lambda r, c: pl.BlockSpec((r, c), lambda i: (0, 0))
    in_specs = [tok_spec(ATTN_WIDTH), tok_spec(CONV_WIDTH), tok_spec(D_MODEL),
                _mod_spec(per_token, 2, TOK_TILE, rows_per_seq), _mod_spec(per_token, 3, TOK_TILE, rows_per_seq),
                _mod_spec(per_token, 4, TOK_TILE, rows_per_seq), _mod_spec(per_token, 5, TOK_TILE, rows_per_seq),
                const(1, D_MODEL), const(D_MODEL, D_MODEL), const(D_MODEL, N_EXPERTS), const(D_MODEL, N_EXPERTS),
                const(D_MODEL, 2 * EXPERT_HIDDEN), const(EXPERT_HIDDEN, D_MODEL)]
    args = [attn, conv, x2d, mod, mod, mod, mod, n2g, wo, wr_hi, wr_lo, wgu, wd]
    aliases = {}
    if carried is not None:
        in_specs += [pl.BlockSpec(memory_space=pl.ANY)] * 3
        aliases = {len(args) + k: k for k in range(3)}
        args += list(carried)
    return pl.pallas_call(
        _out_kernel,
        out_shape=(jax.ShapeDtypeStruct((n_all, D_MODEL), F32), jax.ShapeDtypeStruct((n_all, D_MODEL), F32),
                   jax.ShapeDtypeStruct((n_all, N_EXPERTS), F32)),
        grid=(n_tok // TOK_TILE,),
        in_specs=in_specs,
        out_specs=(out_spec(D_MODEL), out_spec(D_MODEL), out_spec(N_EXPERTS)),
        input_output_aliases=aliases,
        compiler_params=_params(("arbitrary",)),
        name="out_proj",
    )(*args)


def _route_kernel(logit_ref, bias_ref, e_ref, w_ref):
    score = jax.nn.sigmoid(logit_ref[...])
    biased = score + bias_ref[...]
    col = lax.broadcasted_iota(I32, score.shape, 1)
    grp = jnp.right_shift(col, GROUP_SIZE.bit_length() - 1)
    big = N_EXPERTS
    gs = []
    for g in range(N_GROUPS):
        mg = jnp.where(grp == g, biased, NEG_INF)
        m1 = jnp.max(mg, axis=1, keepdims=True)
        i1 = jnp.min(jnp.where(mg == m1, col, big), axis=1, keepdims=True)
        m2 = jnp.max(jnp.where(col == i1, NEG_INF, mg), axis=1, keepdims=True)
        gs.append(m1 + m2)
    kept = jnp.zeros(score.shape, F32)
    for g in range(N_GROUPS):
        rank = jnp.zeros(gs[g].shape, F32)
        for o in range(N_GROUPS):
            if o == g:
                continue
            ahead = (gs[o] >= gs[g]) if o < g else (gs[o] > gs[g])
            rank = rank + jnp.where(ahead, 1.0, 0.0)
        kept = jnp.where(grp == g, jnp.where(rank < TOPK_GROUPS, 1.0, 0.0), kept)
    masked = jnp.where(kept > 0.0, biased, NEG_INF)
    lane = lax.broadcasted_iota(I32, e_ref.shape, 1)
    e_out = jnp.zeros(e_ref.shape, I32)
    w_out = jnp.zeros(w_ref.shape, F32)
    w_sum = jnp.zeros((score.shape[0], 1), F32)
    for k in range(TOP_K):
        m = jnp.max(masked, axis=1, keepdims=True)
        idx = jnp.min(jnp.where(masked == m, col, big), axis=1, keepdims=True)
        pick = col == idx
        wk = jnp.sum(jnp.where(pick, score, 0.0), axis=1, keepdims=True)
        masked = jnp.where(pick, NEG_INF, masked)
        e_out = jnp.where(lane == k, idx, e_out)
        w_out = jnp.where(lane == k, wk, w_out)
        w_sum = w_sum + wk
    e_ref[...] = e_out
    w_ref[...] = w_out / w_sum * ROUTED_SCALE


def _route(logits, bias):
    n_tok = logits.shape[0]
    return pl.pallas_call(
        _route_kernel,
        out_shape=(jax.ShapeDtypeStruct((n_tok, 128), I32), jax.ShapeDtypeStruct((n_tok, 128), F32)),
        grid=(n_tok // TOK_TILE,),
        in_specs=[pl.BlockSpec((TOK_TILE, N_EXPERTS), lambda i: (i, 0)),
                  pl.BlockSpec((1, N_EXPERTS), lambda i: (0, 0))],
        out_specs=(pl.BlockSpec((TOK_TILE, 128), lambda i: (i, 0)), pl.BlockSpec((TOK_TILE, 128), lambda i: (i, 0))),
        compiler_params=_params(("arbitrary",)),
        name="route",
    )(logits, bias)


def _row_copy(src_ref, src_row, dst_ref, dst_row, sem):
    return pltpu.make_async_copy(src_ref.at[pl.ds(src_row, 1), :], dst_ref.at[pl.ds(dst_row, 1), :], sem)


def _gather_kernel(nused_ref, cur_ref, nxt_ref, h_ref, o_ref, xbuf, sem):
    i = pl.program_id(0)
    n_used = nused_ref[0]
    slot = i % 2

    def start(idx_ref, slot_):
        v = idx_ref[...]
        for r in range(MOE_BLOCK):
            _row_copy(h_ref, v[0, r], xbuf.at[slot_], r, sem.at[slot_]).start()

    @pl.when(i == 0)
    def _():
        start(cur_ref, slot)

    @pl.when(i + 1 < n_used)
    def _():
        start(nxt_ref, 1 - slot)

    @pl.when(i < n_used)
    def _():
        for r in range(MOE_BLOCK):
            _row_copy(h_ref, 0, xbuf.at[slot], r, sem.at[slot]).wait()
        o_ref[...] = xbuf[slot].astype(BF16)


def _gather_rows(n_used, row_tok, h_all):
    n_blocks = row_tok.shape[0]
    last = n_blocks - 1
    grid_spec = pltpu.PrefetchScalarGridSpec(
        num_scalar_prefetch=1,
        grid=(n_blocks,),
        in_specs=[pl.BlockSpec((None, 1, MOE_BLOCK), lambda i, nu: (i, 0, 0)),
                  pl.BlockSpec((None, 1, MOE_BLOCK), lambda i, nu: (jnp.minimum(i + 1, last), 0, 0)),
                  pl.BlockSpec(memory_space=pl.ANY)],
        out_specs=pl.BlockSpec((MOE_BLOCK, D_MODEL), lambda i, nu: (jnp.minimum(i, nu[0] - 1), 0)),
        scratch_shapes=[pltpu.VMEM((2, MOE_BLOCK, D_MODEL), F32), pltpu.SemaphoreType.DMA((2,))])
    return pl.pallas_call(
        _gather_kernel,
        out_shape=jax.ShapeDtypeStruct((n_blocks * MOE_BLOCK, D_MODEL), BF16),
        grid_spec=grid_spec,
        compiler_params=_params(("arbitrary",)),
        name="moe_gather",
    )(n_used, row_tok, row_tok, h_all)


def _ffn_kernel(be_ref, nused_ref, x_ref, rw_ref, wg_ref, wu_ref, wd_ref, o_ref):
    @pl.when(pl.program_id(0) < nused_ref[0])
    def _():
        x = x_ref[...]
        gate = _dot(x, wg_ref[0].astype(BF16))
        act = gate * jax.nn.sigmoid(gate) * _dot(x, wu_ref[0].astype(BF16))
        o_ref[...] = _dot(act.astype(BF16), wd_ref[0].astype(BF16)) * rw_ref[...]


def _expert_ffn(blk_expert, n_used, xs, row_w, w_gate, w_up, w_down):
    n_rows = xs.shape[0]
    n_blocks = n_rows // MOE_BLOCK
    live = lambda i, nu: jnp.minimum(i, nu[0] - 1)
    grid_spec = pltpu.PrefetchScalarGridSpec(
        num_scalar_prefetch=2,
        grid=(n_blocks,),
        in_specs=[pl.BlockSpec((MOE_BLOCK, D_MODEL), lambda i, be, nu: (live(i, nu), 0)),
                  pl.BlockSpec((MOE_BLOCK, 1), lambda i, be, nu: (live(i, nu), 0)),
                  pl.BlockSpec((1, D_MODEL, EXPERT_HIDDEN), lambda i, be, nu: (be[live(i, nu)], 0, 0)),
                  pl.BlockSpec((1, D_MODEL, EXPERT_HIDDEN), lambda i, be, nu: (be[live(i, nu)], 0, 0)),
                  pl.BlockSpec((1, EXPERT_HIDDEN, D_MODEL), lambda i, be, nu: (be[live(i, nu)], 0, 0))],
        out_specs=pl.BlockSpec((MOE_BLOCK, D_MODEL), lambda i, be, nu: (live(i, nu), 0)))
    return pl.pallas_call(
        _ffn_kernel,
        out_shape=jax.ShapeDtypeStruct((n_rows, D_MODEL), F32),
        grid_spec=grid_spec,
        compiler_params=_params(("arbitrary",)),
        name="moe_ffn",
    )(blk_expert, n_used, xs, row_w, w_gate, w_up, w_down)


def _combine_kernel(cur_ref, nxt_ref, y_ref, base_ref, g2_ref, o_ref, ybuf, sem):
    i = pl.program_id(0)
    n_steps = pl.num_programs(0)
    slot = i % 2

    def start(pos_ref, slot_):
        v = pos_ref[...]
        for t in range(COMBINE_TILE):
            for k in range(TOP_K):
                _row_copy(y_ref, v[0, t * TOP_K + k], ybuf.at[slot_, k], t, sem.at[slot_]).start()

    @pl.when(i == 0)
    def _():
        start(cur_ref, slot)

    @pl.when(i + 1 < n_steps)
    def _():
        start(nxt_ref, 1 - slot)

    for t in range(COMBINE_TILE):
        for k in range(TOP_K):
            _row_copy(y_ref, 0, ybuf.at[slot, k], t, sem.at[slot]).wait()
    acc = ybuf[slot, 0]
    for k in range(1, TOP_K):
        acc = acc + ybuf[slot, k]
    o_ref[...] = base_ref[...] + g2_ref[...] * acc


def _combine(pos, y_rows, base, mod, per_token, rows_per_seq, n_tok, tile_off):
    n_steps = n_tok // COMBINE_TILE
    last = tile_off + n_steps - 1
    width = COMBINE_TILE * TOP_K
    return pl.pallas_call(
        _combine_kernel,
        out_shape=jax.ShapeDtypeStruct((n_tok, D_MODEL), F32),
        grid=(n_steps,),
        in_specs=[pl.BlockSpec((None, 1, width), lambda i: (i + tile_off, 0, 0)),
                  pl.BlockSpec((None, 1, width), lambda i: (jnp.minimum(i + tile_off + 1, last), 0, 0)),
                  pl.BlockSpec(memory_space=pl.ANY),
                  pl.BlockSpec((COMBINE_TILE, D_MODEL), lambda i: (i + tile_off, 0)),
                  _mod_spec(per_token, 5, COMBINE_TILE, rows_per_seq)],
        out_specs=pl.BlockSpec((COMBINE_TILE, D_MODEL), lambda i: (i, 0)),
        scratch_shapes=[pltpu.VMEM((2, TOP_K, COMBINE_TILE, D_MODEL), F32), pltpu.SemaphoreType.DMA((2,))],
        compiler_params=_params(("arbitrary",)),
        name="moe_combine",
    )(pos, pos, y_rows, base, mod)


def _dispatch_plan(top_e, top_w):
    n_tok = top_e.shape[0]
    n_assign = n_tok * TOP_K
    e_flat = top_e.reshape(-1)
    order = jnp.argsort(e_flat)
    e_sorted = e_flat[order]
    counts = jnp.sum((e_flat[:, None] == jnp.arange(N_EXPERTS, dtype=I32)[None, :]).astype(I32), axis=0)
    padded = (counts + MOE_BLOCK - 1) // MOE_BLOCK * MOE_BLOCK
    pad_end = jnp.cumsum(padded)
    pad_start = pad_end - padded
    grp_start = jnp.cumsum(counts) - counts
    dest = pad_start[e_sorted] + (jnp.arange(n_assign, dtype=I32) - grp_start[e_sorted])
    n_blocks = -(-(n_assign + N_EXPERTS * (MOE_BLOCK - 1)) // MOE_BLOCK)
    n_rows = n_blocks * MOE_BLOCK
    row_tok = jnp.zeros((n_rows,), I32).at[dest].set((order // TOP_K).astype(I32))
    row_w = jnp.zeros((n_rows,), F32).at[dest].set(top_w.reshape(-1)[order])
    pos = jnp.zeros((n_assign,), I32).at[order].set(dest.astype(I32))
    blk_expert = jnp.minimum(jnp.searchsorted(pad_end, jnp.arange(n_blocks, dtype=I32) * MOE_BLOCK, side='right'),
                             N_EXPERTS - 1).astype(I32)
    n_used = (pad_end[-1] // MOE_BLOCK).astype(I32).reshape(1)
    return (row_tok.reshape(n_blocks, 1, MOE_BLOCK), row_w.reshape(n_rows, 1), pos, blk_expert, n_used)


def kernel(x_prompt, x_sample, c_prompt, c_sample, cache_k, cache_v, state_conv, page_table, norm1_g, norm2_g, w_ada, b_ada, w_in, q_norm_g, k_norm_g, conv_w, conv_b, conv_ln_g, conv_ln_b, w_out, w_router, router_bias, w_gate, w_up, w_down, w_shared_gate, w_shared_up, w_shared_down):
    depth = w_in.shape[0]
    assert depth == 1, "single-layer step only"
    bsz, seq, _ = x_prompt.shape
    dbs, t_new, _ = x_sample.shape
    n_pool = cache_k.shape[1]
    n_pages = page_table.shape[1]
    ppb = MOBA_BLOCK // PAGE_SIZE
    assert cache_k.shape[3] == PAGE_SIZE and n_pages % ppb == 0, "past length must be whole MoBA blocks"
    assert seq % TOK_TILE == 0 and (dbs * t_new) % TOK_TILE == 0 and seq // MOBA_BLOCK > MOBA_TOPK
    assert n_pages // ppb >= MOBA_TOPK
    n_p, n_s = bsz * seq, dbs * t_new
    n_all = n_p + n_s
    ly = 0

    n_c = bsz + dbs
    n_c_pad = -(-n_c // 8) * 8
    c_all = jnp.concatenate([c_prompt, c_sample, jnp.zeros((n_c_pad - n_c, D_MODEL), F32)], axis=0)
    mod = _adaln(c_all, w_ada[ly], b_ada[ly][None, :])
    mod_p = mod[:bsz].reshape(bsz, 1, 6 * D_MODEL)
    mod_s = jnp.repeat(mod[bsz:n_c], t_new, axis=0)

    w_in_bf = w_in[ly].astype(BF16)
    qg = jnp.tile(q_norm_g[ly], N_HEADS)[None, :]
    kg = jnp.tile(k_norm_g[ly], N_HEADS)[None, :]
    head_id = jnp.arange(ATTN_WIDTH, dtype=I32) // HEAD_DIM
    head_mean = jnp.where(head_id[:, None] == head_id[None, :], 1.0 / HEAD_DIM, 0.0).astype(BF16)
    n1g = norm1_g[ly][None, :]
    n2g = norm2_g[ly][None, :]
    cw, cb = conv_w[ly], conv_b[ly][None, :]
    ln_g, ln_b = conv_ln_g[ly][None, :], conv_ln_b[ly][None, :]

    xp2 = x_prompt.reshape(n_p, D_MODEL)
    q_p, k_p, v_p, k_pages, v_pages, k_mean, u_p = _project(xp2, mod_p, False, seq, n1g, w_in_bf, qg, kg,
                                                            head_mean, True)
    shp = (bsz, seq, ATTN_WIDTH)
    attn_p = _attn_prompt(q_p.reshape(shp), k_p.reshape(shp), v_p.reshape(shp),
                          k_mean.reshape(bsz, seq // MOBA_BLOCK, ATTN_WIDTH))
    u_p3 = u_p.reshape(bsz, seq, CONV_WIDTH)
    conv_p = _conv_prompt(u_p3, cw, cb, ln_g, ln_b)
    page_shape = (1, bsz, seq // PAGE_SIZE, N_HEADS, PAGE_SIZE, HEAD_DIM)
    k_prompt = k_pages.reshape(page_shape)
    v_prompt = v_pages.reshape(page_shape)
    conv_prompt = u_p3[:, seq - (CONV_KERNEL - 1):, :][None]

    xs2 = x_sample.reshape(n_s, D_MODEL)
    q_s, k_s, v_s, u_s = _project(xs2, mod_s, True, t_new, n1g, w_in_bf, qg, kg, head_mean, False)
    to_heads = lambda t: t.reshape(dbs, t_new, N_HEADS, HEAD_DIM).transpose(0, 2, 1, 3)
    q4, k4, v4 = to_heads(q_s), to_heads(k_s), to_heads(v_s)
    ck_rows = cache_k[ly].reshape(n_pool * N_HEADS, PAGE_SIZE, HEAD_DIM)
    cv_rows = cache_v[ly].reshape(n_pool * N_HEADS, PAGE_SIZE, HEAD_DIM)
    page_sum = _page_sums(ck_rows).reshape(n_pool, ATTN_WIDTH)
    sel = _gate_sample(page_table, page_sum, q_s.reshape(dbs, t_new, ATTN_WIDTH), t_new)
    sel = sel[:, :, :MOBA_TOPK].reshape(dbs, N_HEADS, t_new, MOBA_TOPK)
    logical = jnp.minimum(sel[..., None] * ppb + jnp.arange(ppb, dtype=I32), n_pages - 1)
    phys = page_table[jnp.arange(dbs)[:, None, None, None, None], logical]
    row_idx = phys * N_HEADS + jnp.arange(N_HEADS, dtype=I32)[None, :, None, None, None]
    n_fetch = t_new * MOBA_TOPK * ppb
    attn_s4 = _attn_sample(row_idx.reshape(-1).astype(I32), ck_rows, cv_rows, q4, k4, v4, n_fetch)
    attn_s = attn_s4.transpose(0, 2, 1, 3).reshape(n_s, ATTN_WIDTH).astype(BF16)
    u_s3 = u_s.reshape(dbs, t_new, CONV_WIDTH)
    u_ext = jnp.concatenate([state_conv[ly], u_s3], axis=1)
    conv_s = _conv_sample(u_ext.transpose(1, 0, 2), cw, cb, ln_g, ln_b, t_new)
    conv_s = conv_s.transpose(1, 0, 2).reshape(n_s, CONV_WIDTH)
    k_sample, v_sample = k4[None], v4[None]
    conv_sample = u_ext[:, t_new:, :][None]

    wo = w_out[ly].astype(BF16)
    wr_hi, wr_lo = _split(w_router[ly])
    wgu = jnp.concatenate([w_shared_gate[ly], w_shared_up[ly]], axis=1).astype(BF16)
    wd = w_shared_down[ly].astype(BF16)
    carried = _out_proj(attn_p.reshape(n_p, ATTN_WIDTH), conv_p.reshape(n_p, CONV_WIDTH), xp2, mod_p, False, seq,
                        n2g, wo, wr_hi, wr_lo, wgu, wd, n_all, 0, None)
    base, h_all, logits = _out_proj(attn_s, conv_s, xs2, mod_s, True, t_new, n2g, wo, wr_hi, wr_lo, wgu, wd,
                                    n_all, n_p // TOK_TILE, carried)

    top_e, top_w = _route(logits, router_bias[ly][None, :])
    row_tok, row_w, pos, blk_expert, n_used = _dispatch_plan(top_e[:, :TOP_K], top_w[:, :TOP_K])
    xs_rows = _gather_rows(n_used, row_tok, h_all)
    y_rows = _expert_ffn(blk_expert, n_used, xs_rows, row_w, w_gate[ly], w_up[ly], w_down[ly])
    pos3 = pos.reshape(n_all // COMBINE_TILE, 1, COMBINE_TILE * TOP_K)
    y_p = _combine(pos3, y_rows, base, mod_p, False, seq, n_p, 0)
    y_s = _combine(pos3, y_rows, base, mod_s, True, t_new, n_s, n_p // COMBINE_TILE)

    return (y_p.reshape(bsz, seq, D_MODEL), y_s.reshape(dbs, t_new, D_MODEL), k_prompt, v_prompt, conv_prompt,
            k_sample, v_sample, conv_sample)
```

```python
import functools

import jax
import jax.numpy as jnp
from jax import lax
from jax.experimental import pallas as pl
from jax.experimental.pallas import tpu as pltpu

F32 = jnp.float32
BF16 = jnp.bfloat16
I32 = jnp.int32
U32 = jnp.uint32

D_MODEL = 1024
N_HEADS = 8
HEAD_DIM = 64
ATTN_WIDTH = N_HEADS * HEAD_DIM
CONV_WIDTH = D_MODEL - ATTN_WIDTH
PROJ_WIDTH = 3 * ATTN_WIDTH + 2 * CONV_WIDTH
CONV_KERNEL = 31
MOBA_BLOCK = 256
MOBA_TOPK = 3
PAGE_SIZE = 128
N_EXPERTS = 256
TOP_K = 8
N_GROUPS = 8
TOPK_GROUPS = 4
GROUP_SIZE = N_EXPERTS // N_GROUPS
EXPERT_HIDDEN = 256
ROUTED_SCALE = 2.5
EPS = 1e-6

LANES = 128
HALF = D_MODEL // 2
TOK_TILE = 256
MOE_BLOCK = 128
ROW_TILE = 64
SCORE_CHUNK = 16
VMEM_LIMIT = 48 * 1024 * 1024
NEG_INF = float("-inf")


def _split(a):
    hi = a.astype(BF16)
    lo = (a - hi.astype(F32)).astype(BF16)
    return hi, lo


def _dot(a, b):
    return jnp.dot(a, b, preferred_element_type=F32)


def _dot_nt(a, b):
    return lax.dot_general(a, b, (((1,), (1,)), ((), ())), preferred_element_type=F32)


def _pack_pair(a, b):
    ua = lax.bitcast_convert_type(a.astype(BF16).astype(F32), U32)
    ub = lax.bitcast_convert_type(b.astype(BF16).astype(F32), U32)
    return (ua >> 16) | (ub & jnp.uint32(0xFFFF0000))


def _unpack_pair(w):
    lo = lax.bitcast_convert_type(w << 16, F32)
    hi = lax.bitcast_convert_type(w & jnp.uint32(0xFFFF0000), F32)
    return lo, hi


def _params(sem, vmem=VMEM_LIMIT):
    return pltpu.CompilerParams(dimension_semantics=sem, vmem_limit_bytes=vmem)


def _ada_kernel(c_ref, w_ref, b_ref, o_ref):
    c = c_ref[...]
    s_hi, s_lo = _split(c * jax.nn.sigmoid(c))
    w_hi, w_lo = _split(w_ref[...])
    o_ref[...] = _dot(s_hi, w_hi) + _dot(s_hi, w_lo) + _dot(s_lo, w_hi) + b_ref[...]


def _adaln(c_all, w_ada, b_ada):
    n = c_all.shape[0]
    n_chunks = w_ada.shape[1] // D_MODEL
    return pl.pallas_call(
        _ada_kernel,
        out_shape=jax.ShapeDtypeStruct((n, w_ada.shape[1]), F32),
        grid=(n_chunks,),
        in_specs=[pl.BlockSpec((n, D_MODEL), lambda j: (0, 0)),
                  pl.BlockSpec((D_MODEL, D_MODEL), lambda j: (0, j)),
                  pl.BlockSpec((1, D_MODEL), lambda j: (0, j))],
        out_specs=pl.BlockSpec((n, D_MODEL), lambda j: (0, j)),
        compiler_params=_params(("arbitrary",)),
        name="adaln",
    )(c_all, w_ada, b_ada)


def _seq_mod_spec(chunk, rows, rows_per_seq, n_seq, tile_off=0):
    steps = rows_per_seq // rows
    return pl.BlockSpec((None, 1, D_MODEL),
                        lambda i, *_: (jnp.clip((i - tile_off) // steps, 0, n_seq - 1), 0, chunk))


def _tok_mod_spec(chunk, rows, n_tiles, tile_off=0):
    return pl.BlockSpec((rows, D_MODEL), lambda i, *_: (jnp.clip(i - tile_off, 0, n_tiles - 1), chunk))


def _proj_kernel(x_ref, sh_ref, sc_ref, n1_ref, w_ref, qg_ref, kg_ref, hm_ref, *outs, prompt):
    x = x_ref[...]
    h = x * lax.rsqrt(jnp.mean(x * x, axis=-1, keepdims=True) + EPS) * n1_ref[...]
    h = h * (1.0 + sc_ref[...]) + sh_ref[...]
    p = _dot(h.astype(BF16), w_ref[...])
    hm = hm_ref[...]

    def head_norm(t, g):
        hi, lo = _split(t * t)
        ms = _dot(hi, hm) + _dot(lo, hm)
        return t * lax.rsqrt(ms + EPS) * g

    q = head_norm(p[:, :ATTN_WIDTH], qg_ref[...])
    k = head_norm(p[:, ATTN_WIDTH:2 * ATTN_WIDTH], kg_ref[...])
    v = p[:, 2 * ATTN_WIDTH:3 * ATTN_WIDTH]
    u_a = p[:, 3 * ATTN_WIDTH:3 * ATTN_WIDTH + CONV_WIDTH]
    u_b = p[:, 3 * ATTN_WIDTH + CONV_WIDTH:]
    u = u_a * jax.nn.sigmoid(u_b)
    if prompt:
        qb_ref, kb_ref, vt_ref, kpg_ref, vpg_ref, km_ref, u_ref = outs
        qb_ref[...] = (q * (HEAD_DIM ** -0.5)).astype(BF16)
        kb_ref[...] = k.astype(BF16)
        kt, vt = k.T, v.T
        vt_ref[0] = vt.astype(BF16)
        for pg in range(TOK_TILE // PAGE_SIZE):
            toks = slice(pg * PAGE_SIZE, (pg + 1) * PAGE_SIZE)
            for hd in range(N_HEADS):
                chans = slice(hd * HEAD_DIM, (hd + 1) * HEAD_DIM)
                kpg_ref[pg, hd] = kt[chans, toks]
                vpg_ref[pg, hd] = vt[chans, toks]
        km_ref[...] = jnp.mean(k, axis=0, keepdims=True)
        u_ref[...] = u
    else:
        q_ref, k_ref, v_ref, u_ref = outs
        q_ref[...] = q
        k_ref[...] = k
        v_ref[...] = v
        u_ref[...] = u


def _project(x2d, mod, n_seq, rows_per_seq, norm_g, w_in_bf, qg, kg, head_mean, prompt):
    n_tok = x2d.shape[0]
    n_steps = n_tok // TOK_TILE
    tok = lambda w, dt: jax.ShapeDtypeStruct((n_tok, w), dt)
    tok_spec = lambda w: pl.BlockSpec((TOK_TILE, w), lambda i: (i, 0))
    const = lambda r, c: pl.BlockSpec((r, c), lambda i: (0, 0))
    if prompt:
        n_pages = n_tok // PAGE_SIZE
        ppt = TOK_TILE // PAGE_SIZE
        tps = rows_per_seq // TOK_TILE
        pg_shape = jax.ShapeDtypeStruct((n_pages, N_HEADS, HEAD_DIM, PAGE_SIZE), F32)
        pg_spec = pl.BlockSpec((ppt, N_HEADS, HEAD_DIM, PAGE_SIZE), lambda i: (i, 0, 0, 0))
        out_shape = (tok(ATTN_WIDTH, BF16), tok(ATTN_WIDTH, BF16),
                     jax.ShapeDtypeStruct((n_seq, ATTN_WIDTH, rows_per_seq), BF16), pg_shape, pg_shape,
                     jax.ShapeDtypeStruct((n_steps, 1, ATTN_WIDTH), F32), tok(CONV_WIDTH, F32))
        out_specs = (tok_spec(ATTN_WIDTH), tok_spec(ATTN_WIDTH),
                     pl.BlockSpec((1, ATTN_WIDTH, TOK_TILE), lambda i: (i // tps, 0, i % tps)), pg_spec, pg_spec,
                     pl.BlockSpec((None, 1, ATTN_WIDTH), lambda i: (i, 0, 0)), tok_spec(CONV_WIDTH))
        mods = [_seq_mod_spec(0, TOK_TILE, rows_per_seq, n_seq), _seq_mod_spec(1, TOK_TILE, rows_per_seq, n_seq)]
    else:
        out_shape = (tok(ATTN_WIDTH, F32), tok(ATTN_WIDTH, F32), tok(ATTN_WIDTH, F32), tok(CONV_WIDTH, F32))
        out_specs = (tok_spec(ATTN_WIDTH),) * 3 + (tok_spec(CONV_WIDTH),)
        mods = [_tok_mod_spec(0, TOK_TILE, n_steps), _tok_mod_spec(1, TOK_TILE, n_steps)]
    return pl.pallas_call(
        functools.partial(_proj_kernel, prompt=prompt),
        out_shape=out_shape,
        grid=(n_steps,),
        in_specs=[tok_spec(D_MODEL)] + mods + [
            const(1, D_MODEL), const(D_MODEL, PROJ_WIDTH),
            const(1, ATTN_WIDTH), const(1, ATTN_WIDTH), const(ATTN_WIDTH, ATTN_WIDTH)],
        out_specs=out_specs,
        compiler_params=_params(("arbitrary",)),
        name="proj_prompt" if prompt else "proj_sample",
    )(x2d, mod, mod, norm_g, w_in_bf, qg, kg, head_mean)


def _attn_prompt_kernel(q_ref, k_ref, vt_ref, km_ref, o_ref, neg_ref):
    i = pl.program_id(2)
    n_blk = km_ref.shape[1]
    blk_row = lax.broadcasted_iota(I32, (n_blk, MOBA_BLOCK), 0)
    heads = [slice(hd * HEAD_DIM, (hd + 1) * HEAD_DIM) for hd in range(2)]
    qs = [q_ref[0, :, ls] for ls in heads]
    past = blk_row < i
    for hd, ls in enumerate(heads):
        km_hi, km_lo = _split(km_ref[0, :, ls])
        gate = _dot_nt(km_hi, qs[hd]) + _dot_nt(km_lo, qs[hd])
        g = jnp.where(past, gate, NEG_INF)
        sel = jnp.zeros(g.shape, F32)
        for _ in range(MOBA_TOPK):
            m = jnp.max(g, axis=0, keepdims=True)
            idx = jnp.min(jnp.where(g == m, blk_row, n_blk), axis=0, keepdims=True)
            pick = blk_row == idx
            sel = jnp.where(pick, 1.0, sel)
            g = jnp.where(pick, NEG_INF, g)
        neg_ref[hd] = jnp.where(past, jnp.where(sel > 0.0, 0.0, NEG_INF), NEG_INF)

    key_pos = lax.broadcasted_iota(I32, (MOBA_BLOCK, MOBA_BLOCK), 0)
    qry_pos = lax.broadcasted_iota(I32, (MOBA_BLOCK, MOBA_BLOCK), 1)

    def scores(blk):
        off = pl.multiple_of(blk * MOBA_BLOCK, MOBA_BLOCK)
        return tuple(_dot_nt(k_ref[0, pl.ds(off, MOBA_BLOCK), ls], qs[hd]) for hd, ls in enumerate(heads))

    def absorb(blk, masked, stats):
        off = pl.multiple_of(blk * MOBA_BLOCK, MOBA_BLOCK)
        out = []
        for hd, ls in enumerate(heads):
            m, l, acc = stats[hd]
            s = masked[hd]
            m_new = jnp.maximum(m, jnp.max(s, axis=0, keepdims=True))
            m_ref = jnp.where(m_new == NEG_INF, 0.0, m_new)
            a = jnp.exp(m - m_ref)
            p = jnp.exp(s - m_ref)
            l = a * l + jnp.sum(p, axis=0, keepdims=True)
            acc = a * acc + _dot(vt_ref[0, ls, pl.ds(off, MOBA_BLOCK)], p.astype(BF16))
            out.append((m_new, l, acc))
        return tuple(out)

    stats = tuple((jnp.full((1, MOBA_BLOCK), NEG_INF, F32), jnp.zeros((1, MOBA_BLOCK), F32),
                   jnp.zeros((HEAD_DIM, MOBA_BLOCK), F32)) for _ in heads)

    def body(j, carry):
        s_cur, stats = carry
        s_next = scores(j + 1)
        masked = tuple(s + neg_ref[hd, pl.ds(j, 1), :] for hd, s in enumerate(s_cur))
        return s_next, absorb(j, masked, stats)

    s_own, stats = lax.fori_loop(0, i, body, (scores(0), stats))
    stats = absorb(i, tuple(jnp.where(key_pos <= qry_pos, s, NEG_INF) for s in s_own), stats)
    o_ref[0] = jnp.concatenate([(acc / l).T for _, l, acc in stats], axis=1).astype(BF16)


def _attn_prompt(q, k, vt, kmean):
    bsz, seq, _ = q.shape
    n_blk = seq // MOBA_BLOCK
    pair = 2 * HEAD_DIM
    return pl.pallas_call(
        _attn_prompt_kernel,
        out_shape=jax.ShapeDtypeStruct((bsz, seq, ATTN_WIDTH), BF16),
        grid=(bsz, N_HEADS // 2, n_blk),
        in_specs=[pl.BlockSpec((1, MOBA_BLOCK, pair), lambda b, h, i: (b, i, h)),
                  pl.BlockSpec((1, seq, pair), lambda b, h, i: (b, 0, h)),
                  pl.BlockSpec((1, pair, seq), lambda b, h, i: (b, h, 0)),
                  pl.BlockSpec((1, n_blk, pair), lambda b, h, i: (b, 0, h))],
        out_specs=pl.BlockSpec((1, MOBA_BLOCK, pair), lambda b, h, i: (b, i, h)),
        scratch_shapes=[pltpu.VMEM((2, n_blk, MOBA_BLOCK), F32)],
        compiler_params=_params(("arbitrary", "arbitrary", "arbitrary")),
        name="attn_prompt",
    )(q, k, vt, kmean)


def _ln_swish(y, g, b):
    mean = jnp.mean(y, axis=-1, keepdims=True)
    yc = y - mean
    var = jnp.mean(yc * yc, axis=-1, keepdims=True)
    z = yc * lax.rsqrt(var + EPS) * g + b
    return z * jax.nn.sigmoid(z)


CONV_ROWS = 32
CONV_HALO = 32


def _conv_prompt_kernel(cur_ref, prev_ref, cw_ref, cb_ref, g_ref, b_ref, o_ref, ext_ref):
    i = pl.program_id(1)
    ext_ref[0:CONV_HALO, :] = jnp.where(i > 0, prev_ref[0], 0.0)
    ext_ref[CONV_HALO:, :] = cur_ref[0]
    lead = CONV_HALO - (CONV_KERNEL - 1)
    for c in range(TOK_TILE // CONV_ROWS):
        r0 = c * CONV_ROWS
        acc = jnp.zeros((CONV_ROWS, CONV_WIDTH), F32)
        for j in range(CONV_KERNEL):
            acc = acc + cw_ref[j:j + 1, :] * ext_ref[r0 + lead + j:r0 + lead + j + CONV_ROWS, :]
        y = _ln_swish(acc + cb_ref[...], g_ref[...], b_ref[...])
        o_ref[0, r0:r0 + CONV_ROWS, :] = y.astype(BF16)


def _conv_prompt(u, cw, cb, ln_g, ln_b):
    bsz, seq, _ = u.shape
    halo_per_tile = TOK_TILE // CONV_HALO
    const = lambda r: pl.BlockSpec((r, CONV_WIDTH), lambda b, i: (0, 0))
    return pl.pallas_call(
        _conv_prompt_kernel,
        out_shape=jax.ShapeDtypeStruct((bsz, seq, CONV_WIDTH), BF16),
        grid=(bsz, seq // TOK_TILE),
        in_specs=[pl.BlockSpec((1, TOK_TILE, CONV_WIDTH), lambda b, i: (b, i, 0)),
                  pl.BlockSpec((1, CONV_HALO, CONV_WIDTH),
                               lambda b, i: (b, jnp.maximum(i * halo_per_tile - 1, 0), 0)),
                  const(CONV_KERNEL), const(1), const(1), const(1)],
        out_specs=pl.BlockSpec((1, TOK_TILE, CONV_WIDTH), lambda b, i: (b, i, 0)),
        scratch_shapes=[pltpu.VMEM((CONV_HALO + TOK_TILE, CONV_WIDTH), F32)],
        compiler_params=_params(("arbitrary", "arbitrary")),
        name="conv_prompt",
    )(u, u, cw, cb, ln_g, ln_b)


def _conv_sample_kernel(ext_ref, cw_ref, cb_ref, g_ref, b_ref, o_ref):
    for t in range(o_ref.shape[0]):
        acc = jnp.zeros(o_ref.shape[1:], F32)
        for j in range(CONV_KERNEL):
            acc = acc + cw_ref[j:j + 1, :] * ext_ref[t + j]
        o_ref[t] = _ln_swish(acc + cb_ref[...], g_ref[...], b_ref[...]).astype(BF16)


def _conv_sample(ext_tm, cw, cb, ln_g, ln_b, t_new):
    n_ext, n_seq, _ = ext_tm.shape
    seqs = min(32, n_seq)
    const = lambda r: pl.BlockSpec((r, CONV_WIDTH), lambda i: (0, 0))
    return pl.pallas_call(
        _conv_sample_kernel,
        out_shape=jax.ShapeDtypeStruct((t_new, n_seq, CONV_WIDTH), BF16),
        grid=(n_seq // seqs,),
        in_specs=[pl.BlockSpec((n_ext, seqs, CONV_WIDTH), lambda i: (0, i, 0)),
                  const(CONV_KERNEL), const(1), const(1), const(1)],
        out_specs=pl.BlockSpec((t_new, seqs, CONV_WIDTH), lambda i: (0, i, 0)),
        compiler_params=_params(("arbitrary",)),
        name="conv_sample",
    )(ext_tm, cw, cb, ln_g, ln_b)


def _score_sample_kernel(pt_ref, ck_ref, q_ref, o_ref, kbuf, sem, *, t_new):
    n_chunk = pl.num_programs(1)
    lin = pl.program_id(0) * n_chunk + pl.program_id(1)
    n_lin = pl.num_programs(0) * n_chunk
    slot = lin % 2

    def copies(step, slot_):
        return [pltpu.make_async_copy(ck_ref.at[pt_ref[step * SCORE_CHUNK + p]], kbuf.at[slot_, p], sem.at[slot_])
                for p in range(SCORE_CHUNK)]

    @pl.when(lin == 0)
    def _():
        for c in copies(lin, slot):
            c.start()

    @pl.when(lin + 1 < n_lin)
    def _():
        for c in copies(lin + 1, 1 - slot):
            c.start()

    for c in copies(lin, slot):
        c.wait()
    for p in range(SCORE_CHUNK):
        for hd in range(N_HEADS):
            o_ref[0, hd * t_new:(hd + 1) * t_new, p * PAGE_SIZE:(p + 1) * PAGE_SIZE] = _dot(
                q_ref[0, hd], kbuf[slot, p, hd].astype(BF16))


def _score_sample(page_table, ck_t, q4_bf):
    dbs, n_pages = page_table.shape
    t_new = q4_bf.shape[2]
    rows = N_HEADS * t_new
    grid_spec = pltpu.PrefetchScalarGridSpec(
        num_scalar_prefetch=1,
        grid=(dbs, n_pages // SCORE_CHUNK),
        in_specs=[pl.BlockSpec(memory_space=pl.ANY),
                  pl.BlockSpec((1, N_HEADS, t_new, HEAD_DIM), lambda b, c, pt: (b, 0, 0, 0))],
        out_specs=pl.BlockSpec((1, rows, SCORE_CHUNK * PAGE_SIZE), lambda b, c, pt: (b, 0, c)),
        scratch_shapes=[pltpu.VMEM((2, SCORE_CHUNK, N_HEADS, HEAD_DIM, PAGE_SIZE), F32),
                        pltpu.SemaphoreType.DMA((2,))])
    return pl.pallas_call(
        functools.partial(_score_sample_kernel, t_new=t_new),
        out_shape=jax.ShapeDtypeStruct((dbs, rows, n_pages * PAGE_SIZE), F32),
        grid_spec=grid_spec,
        compiler_params=_params(("arbitrary", "arbitrary")),
        name="score_sample",
    )(page_table.reshape(-1), ck_t, q4_bf)


def _select_sample_kernel(s_ref, q_ref, kn_ref, vn_ref, tq_ref, p_ref, on_ref, st_ref, *, n_blk, t_new):
    rows = s_ref.shape[1]
    lane = lax.broadcasted_iota(I32, (rows, LANES), 1)
    gate = jnp.full((rows, LANES), NEG_INF, F32)
    for n in range(n_blk):
        lo = n * MOBA_BLOCK
        blk = s_ref[0, :, lo:lo + LANES] + s_ref[0, :, lo + LANES:lo + MOBA_BLOCK]
        gate = jnp.where(lane == n, jnp.sum(blk, axis=1, keepdims=True), gate)
    idxs = []
    for _ in range(MOBA_TOPK):
        m = jnp.max(gate, axis=1, keepdims=True)
        idx = jnp.min(jnp.where(gate == m, lane, LANES), axis=1, keepdims=True)
        gate = jnp.where(lane == idx, NEG_INF, gate)
        idxs.append(idx)

    def chosen(n):
        return functools.reduce(jnp.maximum, [jnp.where(ix == n, 1.0, 0.0) for ix in idxs])

    q = q_ref[0]
    tq = tq_ref[:, 0:1]
    s_new = [jnp.where(tq >= float(s), jnp.sum(q * kn_ref[0, s], axis=1, keepdims=True), NEG_INF)
             for s in range(t_new)]
    m = functools.reduce(jnp.maximum, s_new)
    for n in range(n_blk):
        blk = s_ref[0, :, n * MOBA_BLOCK:(n + 1) * MOBA_BLOCK]
        m = jnp.maximum(m, jnp.where(chosen(n) > 0.0, jnp.max(blk, axis=1, keepdims=True), NEG_INF))
    l = jnp.zeros((rows, 1), F32)
    for n in range(n_blk):
        blk = s_ref[0, :, n * MOBA_BLOCK:(n + 1) * MOBA_BLOCK]
        p = jnp.where(chosen(n) > 0.0, jnp.exp(blk - m), 0.0)
        p_ref[0, :, n * MOBA_BLOCK:(n + 1) * MOBA_BLOCK] = p
        l = l + jnp.sum(p, axis=1, keepdims=True)
    o_new = jnp.zeros((rows, HEAD_DIM), F32)
    for s in range(t_new):
        pn = jnp.exp(s_new[s] - m)
        l = l + pn
        o_new = o_new + pn * vn_ref[0, s]
    on_ref[0] = o_new
    st = jnp.where(lane == MOBA_TOPK, l, 0.0)
    for r, ix in enumerate(idxs):
        st = jnp.where(lane == r, ix.astype(F32), st)
    st_ref[0] = st


def _select_sample(scores, q_rows, kn_rows, vn_rows, tq, n_blk, t_new):
    dbs, rows, n_keys = scores.shape
    row_spec = lambda w: pl.BlockSpec((1, rows, w), lambda b: (b, 0, 0))
    new_spec = pl.BlockSpec((1, t_new, rows, HEAD_DIM), lambda b: (b, 0, 0, 0))
    return pl.pallas_call(
        functools.partial(_select_sample_kernel, n_blk=n_blk, t_new=t_new),
        out_shape=(jax.ShapeDtypeStruct((dbs, rows, n_keys), F32),
                   jax.ShapeDtypeStruct((dbs, rows, HEAD_DIM), F32),
                   jax.ShapeDtypeStruct((dbs, rows, LANES), F32)),
        grid=(dbs,),
        in_specs=[row_spec(n_keys), row_spec(HEAD_DIM), new_spec, new_spec,
                  pl.BlockSpec((rows, LANES), lambda b: (0, 0))],
        out_specs=(row_spec(n_keys), row_spec(HEAD_DIM), row_spec(LANES)),
        compiler_params=_params(("arbitrary",)),
        name="select_sample",
    )(scores, q_rows, kn_rows, vn_rows, tq)


def _pv_sample_kernel(pt_ref, lp_ref, cv_ref, p_ref, on_ref, st_ref, o_ref, vbuf, sem, *, n_pages, t_new, per_tok):
    b = pl.program_id(0)
    slot = b % 2
    per_head = t_new * per_tok

    def logical_page(bb, hd, j):
        return lp_ref[(bb * N_HEADS + hd) * per_head + j]

    def copies(bb, slot_):
        out = []
        for hd in range(N_HEADS):
            for j in range(per_head):
                row = pt_ref[bb * n_pages + logical_page(bb, hd, j)] * N_HEADS + hd
                out.append(pltpu.make_async_copy(cv_ref.at[row], vbuf.at[slot_, hd * per_head + j], sem.at[slot_]))
        return out

    @pl.when(b == 0)
    def _():
        for c in copies(b, slot):
            c.start()

    @pl.when(b + 1 < pl.num_programs(0))
    def _():
        for c in copies(b + 1, 1 - slot):
            c.start()

    for c in copies(b, slot):
        c.wait()
    tok = lax.broadcasted_iota(I32, (t_new, 1), 0)
    for hd in range(N_HEADS):
        rows = slice(hd * t_new, (hd + 1) * t_new)
        acc = on_ref[0, rows, :]
        for t in range(t_new):
            ps, vs = [], []
            for j in range(per_tok):
                off = pl.multiple_of(logical_page(b, hd, t * per_tok + j) * PAGE_SIZE, PAGE_SIZE)
                ps.append(p_ref[0, rows, pl.ds(off, PAGE_SIZE)])
                vs.append(vbuf[slot, hd * per_head + t * per_tok + j])
            pm = jnp.where(tok == t, jnp.concatenate(ps, axis=1), 0.0).astype(BF16)
            acc = acc + _dot_nt(pm, jnp.concatenate(vs, axis=1).astype(BF16))
        o_ref[0, rows, :] = acc / st_ref[0, rows, MOBA_TOPK:MOBA_TOPK + 1]


def _pv_sample(page_table, logical_pages, cv_rows, probs, o_new, stats, t_new, per_tok):
    dbs, n_pages = page_table.shape
    rows = probs.shape[1]
    n_fetch = N_HEADS * t_new * per_tok
    row_spec = lambda w: pl.BlockSpec((1, rows, w), lambda b, pt, lp: (b, 0, 0))
    grid_spec = pltpu.PrefetchScalarGridSpec(
        num_scalar_prefetch=2,
        grid=(dbs,),
        in_specs=[pl.BlockSpec(memory_space=pl.ANY), row_spec(probs.shape[2]), row_spec(HEAD_DIM), row_spec(LANES)],
        out_specs=row_spec(HEAD_DIM),
        scratch_shapes=[pltpu.VMEM((2, n_fetch, HEAD_DIM, PAGE_SIZE), F32), pltpu.SemaphoreType.DMA((2,))])
    return pl.pallas_call(
        functools.partial(_pv_sample_kernel, n_pages=n_pages, t_new=t_new, per_tok=per_tok),
        out_shape=jax.ShapeDtypeStruct((dbs, rows, HEAD_DIM), F32),
        grid_spec=grid_spec,
        compiler_params=_params(("arbitrary",)),
        name="pv_sample",
    )(page_table.reshape(-1), logical_pages, cv_rows, probs, o_new, stats)


def _moba_sample(q_s, k_s, v_s, cache_k_l, cache_v_l, page_table):
    dbs, n_pages = page_table.shape
    t_new = q_s.shape[0] // dbs
    ppb = MOBA_BLOCK // PAGE_SIZE
    n_blk = n_pages // ppb
    n_pool = cache_k_l.shape[0]
    rows = N_HEADS * t_new
    scale = HEAD_DIM ** -0.5
    to_heads = lambda t: t.reshape(dbs, t_new, N_HEADS, HEAD_DIM).transpose(0, 2, 1, 3)
    q4, k4, v4 = to_heads(q_s * scale), to_heads(k_s), to_heads(v_s)
    ck_t = jnp.swapaxes(cache_k_l, 2, 3)
    cv_rows = jnp.swapaxes(cache_v_l, 2, 3).reshape(n_pool * N_HEADS, HEAD_DIM, PAGE_SIZE)
    scores = _score_sample(page_table, ck_t, q4.astype(BF16))
    new_rows = lambda t4: jnp.broadcast_to(t4.transpose(0, 2, 1, 3)[:, :, :, None, :],
                                           (dbs, t_new, N_HEADS, t_new, HEAD_DIM)).reshape(dbs, t_new, rows, HEAD_DIM)
    tq = jnp.broadcast_to((jnp.arange(rows, dtype=I32) % t_new).astype(F32)[:, None], (rows, LANES))
    probs, o_new, stats = _select_sample(scores, q4.reshape(dbs, rows, HEAD_DIM), new_rows(k4), new_rows(v4), tq,
                                         n_blk, t_new)
    sel = stats[:, :, :MOBA_TOPK].astype(I32)
    logical = jnp.minimum(sel[..., None] * ppb + jnp.arange(ppb, dtype=I32), n_pages - 1)
    out = _pv_sample(page_table, logical.reshape(-1), cv_rows, probs, o_new, stats, t_new, MOBA_TOPK * ppb)
    return out.reshape(dbs, N_HEADS, t_new, HEAD_DIM).transpose(0, 2, 1, 3).reshape(dbs * t_new, ATTN_WIDTH), k4, v4


def _out_group(attn_ref, conv_ref, x_ref, g1_ref, sh2_ref, sc2_ref, g2_ref, n2_ref, wo_ref, wr_hi_ref, wr_lo_ref,
               wgu_ref, wd_ref, base_ref, hp_ref, logit_ref):
    a = jnp.concatenate([attn_ref[...], conv_ref[...]], axis=1)
    x1 = x_ref[...] + g1_ref[...] * _dot(a, wo_ref[...])
    h = x1 * lax.rsqrt(jnp.mean(x1 * x1, axis=-1, keepdims=True) + EPS) * n2_ref[...]
    h = h * (1.0 + sc2_ref[...]) + sh2_ref[...]
    h_hi, h_lo = _split(h)
    logit_ref[...] = _dot(h_hi, wr_hi_ref[...]) + _dot(h_hi, wr_lo_ref[...]) + _dot(h_lo, wr_hi_ref[...])
    gu = _dot(h_hi, wgu_ref[...])
    gate = gu[:, :EXPERT_HIDDEN]
    act = gate * jax.nn.sigmoid(gate) * gu[:, EXPERT_HIDDEN:]
    base_ref[...] = x1 + g2_ref[...] * _dot(act.astype(BF16), wd_ref[...])
    hp_ref[...] = _pack_pair(h[:, :HALF], h[:, HALF:])


def _out_kernel(*refs, n_first):
    first, second, shared, outs = refs[:7], refs[7:14], refs[14:20], refs[20:]
    i = pl.program_id(0)

    @pl.when(i < n_first)
    def _():
        _out_group(*first, *shared, *outs)

    @pl.when(i >= n_first)
    def _():
        _out_group(*second, *shared, *outs)


def _out_proj(attn_p, conv_p, x_p, mod_p, n_seq_p, seq, attn_s, conv_s, x_s, mod_s, n2g, wo, wr_hi, wr_lo, wgu, wd):
    n_p, n_s = x_p.shape[0], x_s.shape[0]
    tp, ts = n_p // TOK_TILE, n_s // TOK_TILE
    n_all = n_p + n_s
    p_spec = lambda w: pl.BlockSpec((TOK_TILE, w), lambda i: (jnp.minimum(i, tp - 1), 0))
    s_spec = lambda w: pl.BlockSpec((TOK_TILE, w), lambda i: (jnp.maximum(i - tp, 0), 0))
    out_spec = lambda w: pl.BlockSpec((TOK_TILE, w), lambda i: (i, 0))
    const = lambda r, c: pl.BlockSpec((r, c), lambda i: (0, 0))
    in_specs = ([p_spec(ATTN_WIDTH), p_spec(CONV_WIDTH), p_spec(D_MODEL)]
                + [_seq_mod_spec(c, TOK_TILE, seq, n_seq_p) for c in (2, 3, 4, 5)]
                + [s_spec(ATTN_WIDTH), s_spec(CONV_WIDTH), s_spec(D_MODEL)]
                + [_tok_mod_spec(c, TOK_TILE, ts, tp) for c in (2, 3, 4, 5)]
                + [const(1, D_MODEL), const(D_MODEL, D_MODEL), const(D_MODEL, N_EXPERTS), const(D_MODEL, N_EXPERTS),
                   const(D_MODEL, 2 * EXPERT_HIDDEN), const(EXPERT_HIDDEN, D_MODEL)])
    return pl.pallas_call(
        functools.partial(_out_kernel, n_first=tp),
        out_shape=(jax.ShapeDtypeStruct((n_all, D_MODEL), F32), jax.ShapeDtypeStruct((n_all, HALF), U32),
                   jax.ShapeDtypeStruct((n_all, N_EXPERTS), F32)),
        grid=(tp + ts,),
        in_specs=in_specs,
        out_specs=(out_spec(D_MODEL), out_spec(HALF), out_spec(N_EXPERTS)),
        compiler_params=_params(("arbitrary",)),
        name="out_proj",
    )(attn_p, conv_p, x_p, mod_p, mod_p, mod_p, mod_p, attn_s, conv_s, x_s, mod_s, mod_s, mod_s, mod_s,
      n2g, wo, wr_hi, wr_lo, wgu, wd)


def _route_kernel(logit_ref, bias_ref, e_ref, w_ref, cnt_ref):
    score = jax.nn.sigmoid(logit_ref[...])
    biased = score + bias_ref[...]
    col = lax.broadcasted_iota(I32, score.shape, 1)
    grp = jnp.right_shift(col, GROUP_SIZE.bit_length() - 1)
    big = N_EXPERTS
    gs = []
    for g in range(N_GROUPS):
        mg = jnp.where(grp == g, biased, NEG_INF)
        m1 = jnp.max(mg, axis=1, keepdims=True)
        i1 = jnp.min(jnp.where(mg == m1, col, big), axis=1, keepdims=True)
        m2 = jnp.max(jnp.where(col == i1, NEG_INF, mg), axis=1, keepdims=True)
        gs.append(m1 + m2)
    kept = jnp.zeros(score.shape, F32)
    for g in range(N_GROUPS):
        rank = jnp.zeros(gs[g].shape, F32)
        for o in range(N_GROUPS):
            if o == g:
                continue
            ahead = (gs[o] >= gs[g]) if o < g else (gs[o] > gs[g])
            rank = rank + jnp.where(ahead, 1.0, 0.0)
        kept = jnp.where(grp == g, jnp.where(rank < TOPK_GROUPS, 1.0, 0.0), kept)
    masked = jnp.where(kept > 0.0, biased, NEG_INF)
    lane = lax.broadcasted_iota(I32, e_ref.shape, 1)
    e_out = jnp.zeros(e_ref.shape, I32)
    w_out = jnp.zeros(w_ref.shape, F32)
    w_sum = jnp.zeros((score.shape[0], 1), F32)
    hits = jnp.zeros(score.shape, F32)
    for k in range(TOP_K):
        m = jnp.max(masked, axis=1, keepdims=True)
        idx = jnp.min(jnp.where(masked == m, col, big), axis=1, keepdims=True)
        pick = col == idx
        wk = jnp.sum(jnp.where(pick, score, 0.0), axis=1, keepdims=True)
        masked = jnp.where(pick, NEG_INF, masked)
        hits = jnp.where(pick, 1.0, hits)
        e_out = jnp.where(lane == k, idx, e_out)
        w_out = jnp.where(lane == k, wk, w_out)
        w_sum = w_sum + wk
    e_ref[...] = e_out
    w_ref[...] = w_out / w_sum * ROUTED_SCALE

    @pl.when(pl.program_id(0) == 0)
    def _():
        cnt_ref[...] = jnp.zeros(cnt_ref.shape, F32)

    cnt_ref[...] += jnp.sum(hits, axis=0, keepdims=True)


def _route(logits, bias):
    n_tok = logits.shape[0]
    lane_spec = pl.BlockSpec((TOK_TILE, LANES), lambda i: (i, 0))
    return pl.pallas_call(
        _route_kernel,
        out_shape=(jax.ShapeDtypeStruct((n_tok, LANES), I32), jax.ShapeDtypeStruct((n_tok, LANES), F32),
                   jax.ShapeDtypeStruct((1, N_EXPERTS), F32)),
        grid=(n_tok // TOK_TILE,),
        in_specs=[pl.BlockSpec((TOK_TILE, N_EXPERTS), lambda i: (i, 0)),
                  pl.BlockSpec((1, N_EXPERTS), lambda i: (0, 0))],
        out_specs=(lane_spec, lane_spec, pl.BlockSpec((1, N_EXPERTS), lambda i: (0, 0))),
        compiler_params=_params(("arbitrary",)),
        name="route",
    )(logits, bias)


def _plan_kernel(e_ref, start_ref, tri_ref, pos_ref, run_ref):
    @pl.when(pl.program_id(0) == 0)
    def _():
        run_ref[...] = jnp.zeros(run_ref.shape, F32)

    col = lax.broadcasted_iota(I32, (TOK_TILE, N_EXPERTS), 1)
    picks = [col == e_ref[:, k:k + 1] for k in range(TOP_K)]
    hits = jnp.zeros((TOK_TILE, N_EXPERTS), F32)
    for pk in picks:
        hits = jnp.where(pk, 1.0, hits)
    before = _dot(tri_ref[...], hits.astype(BF16)) + run_ref[...] + start_ref[...]
    lane = lax.broadcasted_iota(I32, pos_ref.shape, 1)
    out = jnp.zeros(pos_ref.shape, I32)
    for k, pk in enumerate(picks):
        dest = jnp.sum(jnp.where(pk, before, 0.0), axis=1, keepdims=True)
        out = jnp.where(lane == k, dest.astype(I32), out)
    pos_ref[...] = out
    run_ref[...] += jnp.sum(hits, axis=0, keepdims=True)


def _plan(top_e, group_start):
    n_tok = top_e.shape[0]
    r = jnp.arange(TOK_TILE, dtype=I32)
    tri = (r[None, :] < r[:, None]).astype(BF16)
    return pl.pallas_call(
        _plan_kernel,
        out_shape=jax.ShapeDtypeStruct((n_tok, LANES), I32),
        grid=(n_tok // TOK_TILE,),
        in_specs=[pl.BlockSpec((TOK_TILE, LANES), lambda i: (i, 0)),
                  pl.BlockSpec((1, N_EXPERTS), lambda i: (0, 0)),
                  pl.BlockSpec((TOK_TILE, TOK_TILE), lambda i: (0, 0))],
        out_specs=pl.BlockSpec((TOK_TILE, LANES), lambda i: (i, 0)),
        scratch_shapes=[pltpu.VMEM((1, N_EXPERTS), F32)],
        compiler_params=_params(("arbitrary",)),
        name="moe_plan",
    )(top_e, group_start, tri)


def _row_copy(src_ref, src_row, dst_ref, dst_row, sem):
    return pltpu.make_async_copy(src_ref.at[pl.ds(src_row, 1), :], dst_ref.at[pl.ds(dst_row, 1), :], sem)


def _scatter_kernel(pos_ref, h_ref, init_ref, o_ref, stage, sem):
    del init_ref
    i = pl.program_id(0)
    slot = i % 2
    stage[slot] = h_ref[...]
    v = pos_ref[...]
    for t in range(ROW_TILE):
        for k in range(TOP_K):
            _row_copy(stage.at[slot], t, o_ref, v[0, t * TOP_K + k], sem.at[slot]).start()

    def drain(slot_):
        for _ in range(ROW_TILE * TOP_K):
            _row_copy(stage.at[slot_], 0, o_ref, 0, sem.at[slot_]).wait()

    @pl.when(i > 0)
    def _():
        drain(1 - slot)

    @pl.when(i == pl.num_programs(0) - 1)
    def _():
        drain(slot)


def _scatter_rows(pos3, h_packed, n_rows):
    n_tok = h_packed.shape[0]
    width = ROW_TILE * TOP_K
    zeros = jnp.zeros((n_rows, HALF), U32)
    return pl.pallas_call(
        _scatter_kernel,
        out_shape=jax.ShapeDtypeStruct((n_rows, HALF), U32),
        grid=(n_tok // ROW_TILE,),
        in_specs=[pl.BlockSpec((None, 1, width), lambda i: (i, 0, 0)),
                  pl.BlockSpec((ROW_TILE, HALF), lambda i: (i, 0)),
                  pl.BlockSpec(memory_space=pl.ANY)],
        out_specs=pl.BlockSpec(memory_space=pl.ANY),
        scratch_shapes=[pltpu.VMEM((2, ROW_TILE, HALF), U32), pltpu.SemaphoreType.DMA((2,))],
        input_output_aliases={2: 0},
        compiler_params=_params(("arbitrary",)),
        name="moe_scatter",
    )(pos3, h_packed, zeros)


def _ffn_kernel(be_ref, nused_ref, x_ref, wg_ref, wu_ref, wd_ref, o_ref):
    live = pl.program_id(0) < nused_ref[0]

    @pl.when(live)
    def _():
        x_lo, x_hi = (t.astype(BF16) for t in _unpack_pair(x_ref[...]))
        up = lambda w_ref: (_dot(x_lo, w_ref[0, :HALF, :].astype(BF16)) + _dot(x_hi, w_ref[0, HALF:, :].astype(BF16)))
        gate = up(wg_ref)
        act = (gate * jax.nn.sigmoid(gate) * up(wu_ref)).astype(BF16)
        o_ref[...] = _pack_pair(_dot(act, wd_ref[0, :, :HALF].astype(BF16)), _dot(act, wd_ref[0, :, HALF:].astype(BF16)))

    @pl.when(jnp.logical_not(live))
    def _():
        o_ref[...] = jnp.zeros(o_ref.shape, U32)


def _expert_ffn(blk_expert, n_used, xs, w_gate, w_up, w_down):
    n_rows = xs.shape[0]
    n_blocks = n_rows // MOE_BLOCK
    row_spec = pl.BlockSpec((MOE_BLOCK, HALF), lambda i, be, nu: (i, 0))
    grid_spec = pltpu.PrefetchScalarGridSpec(
        num_scalar_prefetch=2,
        grid=(n_blocks,),
        in_specs=[row_spec,
                  pl.BlockSpec((1, D_MODEL, EXPERT_HIDDEN), lambda i, be, nu: (be[i], 0, 0)),
                  pl.BlockSpec((1, D_MODEL, EXPERT_HIDDEN), lambda i, be, nu: (be[i], 0, 0)),
                  pl.BlockSpec((1, EXPERT_HIDDEN, D_MODEL), lambda i, be, nu: (be[i], 0, 0))],
        out_specs=row_spec)
    return pl.pallas_call(
        _ffn_kernel,
        out_shape=jax.ShapeDtypeStruct((n_rows, HALF), U32),
        grid_spec=grid_spec,
        compiler_params=_params(("arbitrary",)),
        name="moe_ffn",
    )(blk_expert, n_used, xs, w_gate, w_up, w_down)


def _combine_kernel(cur_ref, nxt_ref, y_ref, w_ref, base_ref, g2_ref, o_ref, ybuf, sem):
    i = pl.program_id(0)
    slot = i % 2

    def start(pos_ref, slot_):
        v = pos_ref[...]
        for t in range(ROW_TILE):
            for k in range(TOP_K):
                _row_copy(y_ref, v[0, t * TOP_K + k], ybuf.at[slot_, k], t, sem.at[slot_]).start()

    @pl.when(i == 0)
    def _():
        start(cur_ref, slot)

    @pl.when(i + 1 < pl.num_programs(0))
    def _():
        start(nxt_ref, 1 - slot)

    for _ in range(ROW_TILE * TOP_K):
        _row_copy(y_ref, 0, ybuf.at[slot, 0], 0, sem.at[slot]).wait()
    acc_lo = jnp.zeros((ROW_TILE, HALF), F32)
    acc_hi = jnp.zeros((ROW_TILE, HALF), F32)
    for k in range(TOP_K):
        lo, hi = _unpack_pair(ybuf[slot, k])
        wk = w_ref[:, k:k + 1]
        acc_lo = acc_lo + wk * lo
        acc_hi = acc_hi + wk * hi
    o_ref[:, :HALF] = base_ref[:, :HALF] + g2_ref[:, :HALF] * acc_lo
    o_ref[:, HALF:] = base_ref[:, HALF:] + g2_ref[:, HALF:] * acc_hi


def _combine(pos3, y_rows, top_w, base, g2_spec, mod, n_tok, tile_off):
    n_steps = n_tok // ROW_TILE
    last = tile_off + n_steps - 1
    width = ROW_TILE * TOP_K
    return pl.pallas_call(
        _combine_kernel,
        out_shape=jax.ShapeDtypeStruct((n_tok, D_MODEL), F32),
        grid=(n_steps,),
        in_specs=[pl.BlockSpec((None, 1, width), lambda i: (i + tile_off, 0, 0)),
                  pl.BlockSpec((None, 1, width), lambda i: (jnp.minimum(i + tile_off + 1, last), 0, 0)),
                  pl.BlockSpec(memory_space=pl.ANY),
                  pl.BlockSpec((ROW_TILE, LANES), lambda i: (i + tile_off, 0)),
                  pl.BlockSpec((ROW_TILE, D_MODEL), lambda i: (i + tile_off, 0)),
                  g2_spec],
        out_specs=pl.BlockSpec((ROW_TILE, D_MODEL), lambda i: (i, 0)),
        scratch_shapes=[pltpu.VMEM((2, TOP_K, ROW_TILE, HALF), U32), pltpu.SemaphoreType.DMA((2,))],
        compiler_params=_params(("arbitrary",)),
        name="moe_combine",
    )(pos3, pos3, y_rows, top_w, base, mod)


def _block_table(counts):
    padded = (counts.astype(I32) + MOE_BLOCK - 1) // MOE_BLOCK * MOE_BLOCK
    return padded, jnp.cumsum(padded, axis=1)


def kernel(x_prompt, x_sample, c_prompt, c_sample, cache_k, cache_v, state_conv, page_table, norm1_g, norm2_g, w_ada, b_ada, w_in, q_norm_g, k_norm_g, conv_w, conv_b, conv_ln_g, conv_ln_b, w_out, w_router, router_bias, w_gate, w_up, w_down, w_shared_gate, w_shared_up, w_shared_down):
    depth = w_in.shape[0]
    assert depth == 1, "single-layer step only"
    bsz, seq, _ = x_prompt.shape
    dbs, t_new, _ = x_sample.shape
    n_pages = page_table.shape[1]
    ppb = MOBA_BLOCK // PAGE_SIZE
    n_p, n_s = bsz * seq, dbs * t_new
    n_all = n_p + n_s
    assert cache_k.shape[3] == PAGE_SIZE and n_pages % ppb == 0, "past length must be whole MoBA blocks"
    assert n_pages % SCORE_CHUNK == 0 and MOBA_TOPK <= n_pages // ppb <= LANES
    assert seq % TOK_TILE == 0 and n_s % TOK_TILE == 0 and seq // MOBA_BLOCK > MOBA_TOPK
    assert n_p % ROW_TILE == 0 and n_s % ROW_TILE == 0
    ly = 0

    n_c = bsz + dbs
    n_c_pad = -(-n_c // 8) * 8
    c_all = jnp.concatenate([c_prompt, c_sample, jnp.zeros((n_c_pad - n_c, D_MODEL), F32)], axis=0)
    mod = _adaln(c_all, w_ada[ly], b_ada[ly][None, :])
    mod_p = mod[:bsz].reshape(bsz, 1, 6 * D_MODEL)
    mod_s = jnp.repeat(mod[bsz:n_c], t_new, axis=0)

    w_in_bf = w_in[ly].astype(BF16)
    qg = jnp.tile(q_norm_g[ly], N_HEADS)[None, :]
    kg = jnp.tile(k_norm_g[ly], N_HEADS)[None, :]
    head_id = jnp.arange(ATTN_WIDTH, dtype=I32) // HEAD_DIM
    head_mean = jnp.where(head_id[:, None] == head_id[None, :], 1.0 / HEAD_DIM, 0.0).astype(BF16)
    n1g = norm1_g[ly][None, :]
    n2g = norm2_g[ly][None, :]
    cw, cb = conv_w[ly], conv_b[ly][None, :]
    ln_g, ln_b = conv_ln_g[ly][None, :], conv_ln_b[ly][None, :]

    xp2 = x_prompt.reshape(n_p, D_MODEL)
    q_p, k_p, vt_p, k_pages, v_pages, k_mean, u_p = _project(xp2, mod_p, bsz, seq, n1g, w_in_bf, qg, kg,
                                                             head_mean, True)
    shp = (bsz, seq, ATTN_WIDTH)
    attn_p = _attn_prompt(q_p.reshape(shp), k_p.reshape(shp), vt_p,
                          k_mean.reshape(bsz, seq // MOBA_BLOCK, ATTN_WIDTH))
    u_p3 = u_p.reshape(bsz, seq, CONV_WIDTH)
    conv_p = _conv_prompt(u_p3, cw, cb, ln_g, ln_b)
    page_shape = (1, bsz, seq // PAGE_SIZE, N_HEADS, PAGE_SIZE, HEAD_DIM)
    k_prompt = jnp.swapaxes(k_pages, 2, 3).reshape(page_shape)
    v_prompt = jnp.swapaxes(v_pages, 2, 3).reshape(page_shape)
    conv_prompt = u_p3[:, seq - (CONV_KERNEL - 1):, :][None]

    xs2 = x_sample.reshape(n_s, D_MODEL)
    q_s, k_s, v_s, u_s = _project(xs2, mod_s, dbs, t_new, n1g, w_in_bf, qg, kg, head_mean, False)
    attn_s, k4, v4 = _moba_sample(q_s, k_s, v_s, cache_k[ly], cache_v[ly], page_table)
    u_s3 = u_s.reshape(dbs, t_new, CONV_WIDTH)
    u_ext = jnp.concatenate([state_conv[ly], u_s3], axis=1)
    conv_s = _conv_sample(u_ext.transpose(1, 0, 2), cw, cb, ln_g, ln_b, t_new)
    conv_s = conv_s.transpose(1, 0, 2).reshape(n_s, CONV_WIDTH)
    k_sample, v_sample = k4[None], v4[None]
    conv_sample = u_ext[:, t_new:, :][None]

    wo = w_out[ly].astype(BF16)
    wr_hi, wr_lo = _split(w_router[ly])
    wgu = jnp.concatenate([w_shared_gate[ly], w_shared_up[ly]], axis=1).astype(BF16)
    wd = w_shared_down[ly].astype(BF16)
    base, h_packed, logits = _out_proj(attn_p.reshape(n_p, ATTN_WIDTH), conv_p.reshape(n_p, CONV_WIDTH), xp2, mod_p,
                                       bsz, seq, attn_s.astype(BF16), conv_s, xs2, mod_s, n2g, wo, wr_hi, wr_lo,
                                       wgu, wd)

    top_e, top_w, counts = _route(logits, router_bias[ly][None, :])
    n_assign = n_all * TOP_K
    n_blocks = -(-(n_assign + N_EXPERTS * (MOE_BLOCK - 1)) // MOE_BLOCK)
    n_rows = n_blocks * MOE_BLOCK
    padded, pad_end = _block_table(counts)
    group_start = (pad_end - padded).astype(F32)
    blk_first_row = jnp.arange(n_blocks, dtype=I32)[:, None] * MOE_BLOCK
    blk_expert = jnp.minimum(jnp.sum((pad_end <= blk_first_row).astype(I32), axis=1), N_EXPERTS - 1)
    n_used = (pad_end[0, -1:] // MOE_BLOCK).astype(I32)
    pos = _plan(top_e, group_start)
    pos3 = pos[:, :TOP_K].reshape(n_all // ROW_TILE, 1, ROW_TILE * TOP_K)
    xs_rows = _scatter_rows(pos3, h_packed, n_rows)
    y_rows = _expert_ffn(blk_expert, n_used, xs_rows, w_gate[ly], w_up[ly], w_down[ly])
    g2_p = _seq_mod_spec(5, ROW_TILE, seq, bsz)
    g2_s = _tok_mod_spec(5, ROW_TILE, n_s // ROW_TILE)
    y_p = _combine(pos3, y_rows, top_w, base, g2_p, mod_p, n_p, 0)
    y_s = _combine(pos3, y_rows, top_w, base, g2_s, mod_s, n_s, n_p // ROW_TILE)

    return (y_p.reshape(bsz, seq, D_MODEL), y_s.reshape(dbs, t_new, D_MODEL), k_prompt, v_prompt, conv_prompt,
            k_sample, v_sample, conv_sample)
```

```python
import functools

import jax
import jax.numpy as jnp
from jax import lax
from jax.experimental import pallas as pl
from jax.experimental.pallas import tpu as pltpu

F32 = jnp.float32
BF16 = jnp.bfloat16
I32 = jnp.int32

D_MODEL = 1024
N_HEADS = 8
HEAD_DIM = 64
ATTN_WIDTH = N_HEADS * HEAD_DIM
CONV_WIDTH = D_MODEL - ATTN_WIDTH
PROJ_WIDTH = 3 * ATTN_WIDTH + 2 * CONV_WIDTH
CONV_KERNEL = 31
MOBA_BLOCK = 256
MOBA_TOPK = 3
PAGE_SIZE = 128
N_EXPERTS = 256
TOP_K = 8
N_GROUPS = 8
TOPK_GROUPS = 4
GROUP_SIZE = N_EXPERTS // N_GROUPS
EXPERT_HIDDEN = 256
ROUTED_SCALE = 2.5
EPS = 1e-6

LANES = 128
ROW_PARTS = D_MODEL // LANES
TOK_TILE = 256
MOE_BLOCK = 128
ATTN_HEADS = 4
ROW_TILE = 64
SCORE_CHUNK = 32
VMEM_LIMIT = 48 * 1024 * 1024
NEG_INF = float("-inf")


def _split(a):
    hi = a.astype(BF16)
    lo = (a - hi.astype(F32)).astype(BF16)
    return hi, lo


def _dot(a, b):
    return jnp.dot(a, b, preferred_element_type=F32)


def _dot_nt(a, b):
    return lax.dot_general(a, b, (((1,), (1,)), ((), ())), preferred_element_type=F32)


def _params(sem, vmem=VMEM_LIMIT):
    return pltpu.CompilerParams(dimension_semantics=sem, vmem_limit_bytes=vmem)


def _ada_kernel(c_ref, w_ref, b_ref, o_ref):
    c = c_ref[...]
    s_hi, s_lo = _split(c * jax.nn.sigmoid(c))
    w_hi, w_lo = _split(w_ref[...])
    o_ref[...] = _dot(s_hi, w_hi) + _dot(s_hi, w_lo) + _dot(s_lo, w_hi) + b_ref[...]


def _adaln(c_all, w_ada, b_ada):
    n = c_all.shape[0]
    n_chunks = w_ada.shape[1] // D_MODEL
    return pl.pallas_call(
        _ada_kernel,
        out_shape=jax.ShapeDtypeStruct((n, w_ada.shape[1]), F32),
        grid=(n_chunks,),
        in_specs=[pl.BlockSpec((n, D_MODEL), lambda j: (0, 0)),
                  pl.BlockSpec((D_MODEL, D_MODEL), lambda j: (0, j)),
                  pl.BlockSpec((1, D_MODEL), lambda j: (0, j))],
        out_specs=pl.BlockSpec((n, D_MODEL), lambda j: (0, j)),
        compiler_params=_params(("arbitrary",)),
        name="adaln",
    )(c_all, w_ada, b_ada)


def _seq_mod_spec(chunk, rows, rows_per_seq, n_seq, tile_off=0):
    steps = rows_per_seq // rows
    return pl.BlockSpec((None, 1, D_MODEL),
                        lambda i, *_: (jnp.clip((i - tile_off) // steps, 0, n_seq - 1), 0, chunk))


def _tok_mod_spec(chunk, rows, n_tiles, tile_off=0):
    return pl.BlockSpec((rows, D_MODEL), lambda i, *_: (jnp.clip(i - tile_off, 0, n_tiles - 1), chunk))


def _proj_kernel(x_ref, sh_ref, sc_ref, n1_ref, w_ref, qg_ref, kg_ref, hm_ref, *outs, prompt):
    x = x_ref[...]
    h = x * lax.rsqrt(jnp.mean(x * x, axis=-1, keepdims=True) + EPS) * n1_ref[...]
    h = h * (1.0 + sc_ref[...]) + sh_ref[...]
    p = _dot(h.astype(BF16), w_ref[...])
    hm = hm_ref[...]

    def head_norm(t, g):
        hi, lo = _split(t * t)
        ms = _dot(hi, hm) + _dot(lo, hm)
        return t * lax.rsqrt(ms + EPS) * g

    q = head_norm(p[:, :ATTN_WIDTH], qg_ref[...])
    k = head_norm(p[:, ATTN_WIDTH:2 * ATTN_WIDTH], kg_ref[...])
    v = p[:, 2 * ATTN_WIDTH:3 * ATTN_WIDTH]
    u_a = p[:, 3 * ATTN_WIDTH:3 * ATTN_WIDTH + CONV_WIDTH]
    u_b = p[:, 3 * ATTN_WIDTH + CONV_WIDTH:]
    u = u_a * jax.nn.sigmoid(u_b)
    if prompt:
        qb_ref, kb_ref, vt_ref, kpg_ref, vpg_ref, km_ref, u_ref = outs
        qb_ref[...] = (q * (HEAD_DIM ** -0.5)).astype(BF16)
        kb_ref[...] = k.astype(BF16)
        kt, vt = k.T, v.T
        vt_ref[0] = vt.astype(BF16)
        for pg in range(TOK_TILE // PAGE_SIZE):
            toks = slice(pg * PAGE_SIZE, (pg + 1) * PAGE_SIZE)
            for hd in range(N_HEADS):
                chans = slice(hd * HEAD_DIM, (hd + 1) * HEAD_DIM)
                kpg_ref[pg, hd] = kt[chans, toks]
                vpg_ref[pg, hd] = vt[chans, toks]
        km_ref[...] = jnp.mean(k, axis=0, keepdims=True)
        u_ref[...] = u
    else:
        q_ref, k_ref, v_ref, u_ref = outs
        q_ref[...] = q
        k_ref[...] = k
        v_ref[...] = v
        u_ref[...] = u


def _project(x2d, mod, n_seq, rows_per_seq, norm_g, w_in_bf, qg, kg, head_mean, prompt):
    n_tok = x2d.shape[0]
    n_steps = n_tok // TOK_TILE
    tok = lambda w, dt: jax.ShapeDtypeStruct((n_tok, w), dt)
    tok_spec = lambda w: pl.BlockSpec((TOK_TILE, w), lambda i: (i, 0))
    const = lambda r, c: pl.BlockSpec((r, c), lambda i: (0, 0))
    if prompt:
        n_pages = n_tok // PAGE_SIZE
        ppt = TOK_TILE // PAGE_SIZE
        tps = rows_per_seq // TOK_TILE
        pg_shape = jax.ShapeDtypeStruct((n_pages, N_HEADS, HEAD_DIM, PAGE_SIZE), F32)
        pg_spec = pl.BlockSpec((ppt, N_HEADS, HEAD_DIM, PAGE_SIZE), lambda i: (i, 0, 0, 0))
        out_shape = (tok(ATTN_WIDTH, BF16), tok(ATTN_WIDTH, BF16),
                     jax.ShapeDtypeStruct((n_seq, ATTN_WIDTH, rows_per_seq), BF16), pg_shape, pg_shape,
                     jax.ShapeDtypeStruct((n_steps, 1, ATTN_WIDTH), F32), tok(CONV_WIDTH, F32))
        out_specs = (tok_spec(ATTN_WIDTH), tok_spec(ATTN_WIDTH),
                     pl.BlockSpec((1, ATTN_WIDTH, TOK_TILE), lambda i: (i // tps, 0, i % tps)), pg_spec, pg_spec,
                     pl.BlockSpec((None, 1, ATTN_WIDTH), lambda i: (i, 0, 0)), tok_spec(CONV_WIDTH))
        mods = [_seq_mod_spec(0, TOK_TILE, rows_per_seq, n_seq), _seq_mod_spec(1, TOK_TILE, rows_per_seq, n_seq)]
    else:
        out_shape = (tok(ATTN_WIDTH, F32), tok(ATTN_WIDTH, F32), tok(ATTN_WIDTH, F32), tok(CONV_WIDTH, F32))
        out_specs = (tok_spec(ATTN_WIDTH),) * 3 + (tok_spec(CONV_WIDTH),)
        mods = [_tok_mod_spec(0, TOK_TILE, n_steps), _tok_mod_spec(1, TOK_TILE, n_steps)]
    return pl.pallas_call(
        functools.partial(_proj_kernel, prompt=prompt),
        out_shape=out_shape,
        grid=(n_steps,),
        in_specs=[tok_spec(D_MODEL)] + mods + [
            const(1, D_MODEL), const(D_MODEL, PROJ_WIDTH),
            const(1, ATTN_WIDTH), const(1, ATTN_WIDTH), const(ATTN_WIDTH, ATTN_WIDTH)],
        out_specs=out_specs,
        compiler_params=_params(("arbitrary",)),
        name="proj_prompt" if prompt else "proj_sample",
    )(x2d, mod, mod, norm_g, w_in_bf, qg, kg, head_mean)


def _attn_prompt_kernel(q_ref, k_ref, vt_ref, km_ref, o_ref, neg_ref):
    i = pl.program_id(2)
    n_blk = km_ref.shape[1]
    blk_row = lax.broadcasted_iota(I32, (n_blk, MOBA_BLOCK), 0)
    heads = [slice(hd * HEAD_DIM, (hd + 1) * HEAD_DIM) for hd in range(ATTN_HEADS)]
    qs = [q_ref[0, :, ls] for ls in heads]
    past = blk_row < i
    for hd, ls in enumerate(heads):
        km_hi, km_lo = _split(km_ref[0, :, ls])
        gate = _dot_nt(km_hi, qs[hd]) + _dot_nt(km_lo, qs[hd])
        g = jnp.where(past, gate, NEG_INF)
        sel = jnp.zeros(g.shape, F32)
        for _ in range(MOBA_TOPK):
            m = jnp.max(g, axis=0, keepdims=True)
            idx = jnp.min(jnp.where(g == m, blk_row, n_blk), axis=0, keepdims=True)
            pick = blk_row == idx
            sel = jnp.where(pick, 1.0, sel)
            g = jnp.where(pick, NEG_INF, g)
        neg_ref[hd] = jnp.where(past, jnp.where(sel > 0.0, 0.0, NEG_INF), NEG_INF)

    key_pos = lax.broadcasted_iota(I32, (MOBA_BLOCK, MOBA_BLOCK), 0)
    qry_pos = lax.broadcasted_iota(I32, (MOBA_BLOCK, MOBA_BLOCK), 1)

    def scores(blk):
        off = pl.multiple_of(blk * MOBA_BLOCK, MOBA_BLOCK)
        return tuple(_dot_nt(k_ref[0, pl.ds(off, MOBA_BLOCK), ls], qs[hd]) for hd, ls in enumerate(heads))

    def absorb(blk, masked, stats):
        off = pl.multiple_of(blk * MOBA_BLOCK, MOBA_BLOCK)
        out = []
        for hd, ls in enumerate(heads):
            m, l, acc = stats[hd]
            s = masked[hd]
            m_new = jnp.maximum(m, jnp.max(s, axis=0, keepdims=True))
            m_ref = jnp.where(m_new == NEG_INF, 0.0, m_new)
            a = jnp.exp(m - m_ref)
            p = jnp.exp(s - m_ref)
            l = a * l + jnp.sum(p, axis=0, keepdims=True)
            acc = a * acc + _dot(vt_ref[0, ls, pl.ds(off, MOBA_BLOCK)], p.astype(BF16))
            out.append((m_new, l, acc))
        return tuple(out)

    stats = tuple((jnp.full((1, MOBA_BLOCK), NEG_INF, F32), jnp.zeros((1, MOBA_BLOCK), F32),
                   jnp.zeros((HEAD_DIM, MOBA_BLOCK), F32)) for _ in heads)

    def body(j, carry):
        s_cur, stats = carry
        s_next = scores(j + 1)
        masked = tuple(s + neg_ref[hd, pl.ds(j, 1), :] for hd, s in enumerate(s_cur))
        return s_next, absorb(j, masked, stats)

    s_own, stats = lax.fori_loop(0, i, body, (scores(0), stats))
    stats = absorb(i, tuple(jnp.where(key_pos <= qry_pos, s, NEG_INF) for s in s_own), stats)
    o_ref[0] = jnp.concatenate([(acc / l).T for _, l, acc in stats], axis=1).astype(BF16)


def _attn_prompt(q, k, vt, kmean):
    bsz, seq, _ = q.shape
    n_blk = seq // MOBA_BLOCK
    pair = ATTN_HEADS * HEAD_DIM
    return pl.pallas_call(
        _attn_prompt_kernel,
        out_shape=jax.ShapeDtypeStruct((bsz, seq, ATTN_WIDTH), BF16),
        grid=(bsz, N_HEADS // ATTN_HEADS, n_blk),
        in_specs=[pl.BlockSpec((1, MOBA_BLOCK, pair), lambda b, h, i: (b, i, h)),
                  pl.BlockSpec((1, seq, pair), lambda b, h, i: (b, 0, h)),
                  pl.BlockSpec((1, pair, seq), lambda b, h, i: (b, h, 0)),
                  pl.BlockSpec((1, n_blk, pair), lambda b, h, i: (b, 0, h))],
        out_specs=pl.BlockSpec((1, MOBA_BLOCK, pair), lambda b, h, i: (b, i, h)),
        scratch_shapes=[pltpu.VMEM((ATTN_HEADS, n_blk, MOBA_BLOCK), F32)],
        compiler_params=_params(("arbitrary", "arbitrary", "arbitrary")),
        name="attn_prompt",
    )(q, k, vt, kmean)


def _ln_swish(y, g, b):
    mean = jnp.mean(y, axis=-1, keepdims=True)
    yc = y - mean
    var = jnp.mean(yc * yc, axis=-1, keepdims=True)
    z = yc * lax.rsqrt(var + EPS) * g + b
    return z * jax.nn.sigmoid(z)


CONV_ROWS = 32
CONV_HALO = 32


def _conv_prompt_kernel(cur_ref, prev_ref, cw_ref, cb_ref, g_ref, b_ref, o_ref, ext_ref):
    i = pl.program_id(1)
    ext_ref[0:CONV_HALO, :] = jnp.where(i > 0, prev_ref[0], 0.0)
    ext_ref[CONV_HALO:, :] = cur_ref[0]
    lead = CONV_HALO - (CONV_KERNEL - 1)
    for c in range(TOK_TILE // CONV_ROWS):
        r0 = c * CONV_ROWS
        acc = jnp.zeros((CONV_ROWS, CONV_WIDTH), F32)
        for j in range(CONV_KERNEL):
            acc = acc + cw_ref[j:j + 1, :] * ext_ref[r0 + lead + j:r0 + lead + j + CONV_ROWS, :]
        y = _ln_swish(acc + cb_ref[...], g_ref[...], b_ref[...])
        o_ref[0, r0:r0 + CONV_ROWS, :] = y.astype(BF16)


def _conv_prompt(u, cw, cb, ln_g, ln_b):
    bsz, seq, _ = u.shape
    halo_per_tile = TOK_TILE // CONV_HALO
    const = lambda r: pl.BlockSpec((r, CONV_WIDTH), lambda b, i: (0, 0))
    return pl.pallas_call(
        _conv_prompt_kernel,
        out_shape=jax.ShapeDtypeStruct((bsz, seq, CONV_WIDTH), BF16),
        grid=(bsz, seq // TOK_TILE),
        in_specs=[pl.BlockSpec((1, TOK_TILE, CONV_WIDTH), lambda b, i: (b, i, 0)),
                  pl.BlockSpec((1, CONV_HALO, CONV_WIDTH),
                               lambda b, i: (b, jnp.maximum(i * halo_per_tile - 1, 0), 0)),
                  const(CONV_KERNEL), const(1), const(1), const(1)],
        out_specs=pl.BlockSpec((1, TOK_TILE, CONV_WIDTH), lambda b, i: (b, i, 0)),
        scratch_shapes=[pltpu.VMEM((CONV_HALO + TOK_TILE, CONV_WIDTH), F32)],
        compiler_params=_params(("arbitrary", "arbitrary")),
        name="conv_prompt",
    )(u, u, cw, cb, ln_g, ln_b)


def _conv_sample_kernel(ext_ref, cw_ref, cb_ref, g_ref, b_ref, o_ref):
    for t in range(o_ref.shape[0]):
        acc = jnp.zeros(o_ref.shape[1:], F32)
        for j in range(CONV_KERNEL):
            acc = acc + cw_ref[j:j + 1, :] * ext_ref[t + j]
        o_ref[t] = _ln_swish(acc + cb_ref[...], g_ref[...], b_ref[...]).astype(BF16)


def _conv_sample(ext_tm, cw, cb, ln_g, ln_b, t_new):
    n_ext, n_seq, _ = ext_tm.shape
    seqs = min(32, n_seq)
    const = lambda r: pl.BlockSpec((r, CONV_WIDTH), lambda i: (0, 0))
    return pl.pallas_call(
        _conv_sample_kernel,
        out_shape=jax.ShapeDtypeStruct((t_new, n_seq, CONV_WIDTH), BF16),
        grid=(n_seq // seqs,),
        in_specs=[pl.BlockSpec((n_ext, seqs, CONV_WIDTH), lambda i: (0, i, 0)),
                  const(CONV_KERNEL), const(1), const(1), const(1)],
        out_specs=pl.BlockSpec((t_new, seqs, CONV_WIDTH), lambda i: (0, i, 0)),
        compiler_params=_params(("arbitrary",)),
        name="conv_sample",
    )(ext_tm, cw, cb, ln_g, ln_b)


def _score_sample_kernel(pt_ref, ck_ref, q_ref, o_ref, kbuf, sem, *, t_new):
    n_chunk = pl.num_programs(1)
    lin = pl.program_id(0) * n_chunk + pl.program_id(1)
    n_lin = pl.num_programs(0) * n_chunk
    slot = lin % 2

    def copies(step, slot_):
        return [pltpu.make_async_copy(ck_ref.at[pt_ref[step * SCORE_CHUNK + p]], kbuf.at[slot_, p], sem.at[slot_])
                for p in range(SCORE_CHUNK)]

    @pl.when(lin == 0)
    def _():
        for c in copies(lin, slot):
            c.start()

    @pl.when(lin + 1 < n_lin)
    def _():
        for c in copies(lin + 1, 1 - slot):
            c.start()

    for c in copies(lin, slot):
        c.wait()
    for p in range(SCORE_CHUNK):
        for hd in range(N_HEADS):
            o_ref[0, hd * t_new:(hd + 1) * t_new, p * PAGE_SIZE:(p + 1) * PAGE_SIZE] = _dot(
                q_ref[0, hd], kbuf[slot, p, hd].astype(BF16))


def _score_sample(page_table, ck_t, q4_bf):
    dbs, n_pages = page_table.shape
    t_new = q4_bf.shape[2]
    rows = N_HEADS * t_new
    grid_spec = pltpu.PrefetchScalarGridSpec(
        num_scalar_prefetch=1,
        grid=(dbs, n_pages // SCORE_CHUNK),
        in_specs=[pl.BlockSpec(memory_space=pl.ANY),
                  pl.BlockSpec((1, N_HEADS, t_new, HEAD_DIM), lambda b, c, pt: (b, 0, 0, 0))],
        out_specs=pl.BlockSpec((1, rows, SCORE_CHUNK * PAGE_SIZE), lambda b, c, pt: (b, 0, c)),
        scratch_shapes=[pltpu.VMEM((2, SCORE_CHUNK, N_HEADS, HEAD_DIM, PAGE_SIZE), F32),
                        pltpu.SemaphoreType.DMA((2,))])
    return pl.pallas_call(
        functools.partial(_score_sample_kernel, t_new=t_new),
        out_shape=jax.ShapeDtypeStruct((dbs, rows, n_pages * PAGE_SIZE), F32),
        grid_spec=grid_spec,
        compiler_params=_params(("arbitrary", "arbitrary")),
        name="score_sample",
    )(page_table.reshape(-1), ck_t, q4_bf)


def _select_sample_kernel(s_ref, q_ref, kn_ref, vn_ref, tq_ref, p_ref, on_ref, st_ref, *, n_blk, t_new):
    rows = s_ref.shape[1]
    lane = lax.broadcasted_iota(I32, (rows, LANES), 1)
    gate = jnp.full((rows, LANES), NEG_INF, F32)
    for n in range(n_blk):
        lo = n * MOBA_BLOCK
        blk = s_ref[0, :, lo:lo + LANES] + s_ref[0, :, lo + LANES:lo + MOBA_BLOCK]
        gate = jnp.where(lane == n, jnp.sum(blk, axis=1, keepdims=True), gate)
    idxs = []
    for _ in range(MOBA_TOPK):
        m = jnp.max(gate, axis=1, keepdims=True)
        idx = jnp.min(jnp.where(gate == m, lane, LANES), axis=1, keepdims=True)
        gate = jnp.where(lane == idx, NEG_INF, gate)
        idxs.append(idx)

    def chosen(n):
        return functools.reduce(jnp.maximum, [jnp.where(ix == n, 1.0, 0.0) for ix in idxs])

    q = q_ref[0]
    tq = tq_ref[:, 0:1]
    s_new = [jnp.where(tq >= float(s), jnp.sum(q * kn_ref[0, s], axis=1, keepdims=True), NEG_INF)
             for s in range(t_new)]
    m = functools.reduce(jnp.maximum, s_new)
    for n in range(n_blk):
        blk = s_ref[0, :, n * MOBA_BLOCK:(n + 1) * MOBA_BLOCK]
        m = jnp.maximum(m, jnp.where(chosen(n) > 0.0, jnp.max(blk, axis=1, keepdims=True), NEG_INF))
    l = jnp.zeros((rows, 1), F32)
    for n in range(n_blk):
        blk = s_ref[0, :, n * MOBA_BLOCK:(n + 1) * MOBA_BLOCK]
        p = jnp.where(chosen(n) > 0.0, jnp.exp(blk - m), 0.0)
        p_ref[0, :, n * MOBA_BLOCK:(n + 1) * MOBA_BLOCK] = p
        l = l + jnp.sum(p, axis=1, keepdims=True)
    o_new = jnp.zeros((rows, HEAD_DIM), F32)
    for s in range(t_new):
        pn = jnp.exp(s_new[s] - m)
        l = l + pn
        o_new = o_new + pn * vn_ref[0, s]
    on_ref[0] = o_new
    st = jnp.where(lane == MOBA_TOPK, l, 0.0)
    for r, ix in enumerate(idxs):
        st = jnp.where(lane == r, ix.astype(F32), st)
    st_ref[0] = st


def _select_sample(scores, q_rows, kn_rows, vn_rows, tq, n_blk, t_new):
    dbs, rows, n_keys = scores.shape
    row_spec = lambda w: pl.BlockSpec((1, rows, w), lambda b: (b, 0, 0))
    new_spec = pl.BlockSpec((1, t_new, rows, HEAD_DIM), lambda b: (b, 0, 0, 0))
    return pl.pallas_call(
        functools.partial(_select_sample_kernel, n_blk=n_blk, t_new=t_new),
        out_shape=(jax.ShapeDtypeStruct((dbs, rows, n_keys), F32),
                   jax.ShapeDtypeStruct((dbs, rows, HEAD_DIM), F32),
                   jax.ShapeDtypeStruct((dbs, rows, LANES), F32)),
        grid=(dbs,),
        in_specs=[row_spec(n_keys), row_spec(HEAD_DIM), new_spec, new_spec,
                  pl.BlockSpec((rows, LANES), lambda b: (0, 0))],
        out_specs=(row_spec(n_keys), row_spec(HEAD_DIM), row_spec(LANES)),
        compiler_params=_params(("arbitrary",)),
        name="select_sample",
    )(scores, q_rows, kn_rows, vn_rows, tq)


def _pv_sample_kernel(pt_ref, lp_ref, cv_ref, p_ref, on_ref, st_ref, o_ref, vbuf, sem, *, n_pages, t_new, per_tok):
    b = pl.program_id(0)
    slot = b % 2
    per_head = t_new * per_tok

    def logical_page(bb, hd, j):
        return lp_ref[(bb * N_HEADS + hd) * per_head + j]

    def copies(bb, slot_):
        out = []
        for hd in range(N_HEADS):
            for j in range(per_head):
                row = pt_ref[bb * n_pages + logical_page(bb, hd, j)] * N_HEADS + hd
                out.append(pltpu.make_async_copy(cv_ref.at[row], vbuf.at[slot_, hd * per_head + j], sem.at[slot_]))
        return out

    @pl.when(b == 0)
    def _():
        for c in copies(b, slot):
            c.start()

    @pl.when(b + 1 < pl.num_programs(0))
    def _():
        for c in copies(b + 1, 1 - slot):
            c.start()

    for c in copies(b, slot):
        c.wait()
    tok = lax.broadcasted_iota(I32, (t_new, 1), 0)
    for hd in range(N_HEADS):
        rows = slice(hd * t_new, (hd + 1) * t_new)
        acc = on_ref[0, rows, :]
        for t in range(t_new):
            ps, vs = [], []
            for j in range(per_tok):
                off = pl.multiple_of(logical_page(b, hd, t * per_tok + j) * PAGE_SIZE, PAGE_SIZE)
                ps.append(p_ref[0, rows, pl.ds(off, PAGE_SIZE)])
                vs.append(vbuf[slot, hd * per_head + t * per_tok + j])
            pm = jnp.where(tok == t, jnp.concatenate(ps, axis=1), 0.0).astype(BF16)
            acc = acc + _dot_nt(pm, jnp.concatenate(vs, axis=1).astype(BF16))
        o_ref[0, rows, :] = acc / st_ref[0, rows, MOBA_TOPK:MOBA_TOPK + 1]


def _pv_sample(page_table, logical_pages, cv_rows, probs, o_new, stats, t_new, per_tok):
    dbs, n_pages = page_table.shape
    rows = probs.shape[1]
    n_fetch = N_HEADS * t_new * per_tok
    row_spec = lambda w: pl.BlockSpec((1, rows, w), lambda b, pt, lp: (b, 0, 0))
    grid_spec = pltpu.PrefetchScalarGridSpec(
        num_scalar_prefetch=2,
        grid=(dbs,),
        in_specs=[pl.BlockSpec(memory_space=pl.ANY), row_spec(probs.shape[2]), row_spec(HEAD_DIM), row_spec(LANES)],
        out_specs=row_spec(HEAD_DIM),
        scratch_shapes=[pltpu.VMEM((2, n_fetch, HEAD_DIM, PAGE_SIZE), F32), pltpu.SemaphoreType.DMA((2,))])
    return pl.pallas_call(
        functools.partial(_pv_sample_kernel, n_pages=n_pages, t_new=t_new, per_tok=per_tok),
        out_shape=jax.ShapeDtypeStruct((dbs, rows, HEAD_DIM), F32),
        grid_spec=grid_spec,
        compiler_params=_params(("arbitrary",)),
        name="pv_sample",
    )(page_table.reshape(-1), logical_pages, cv_rows, probs, o_new, stats)


def _moba_sample(q_s, k_s, v_s, cache_k_l, cache_v_l, page_table):
    dbs, n_pages = page_table.shape
    t_new = q_s.shape[0] // dbs
    ppb = MOBA_BLOCK // PAGE_SIZE
    n_blk = n_pages // ppb
    n_pool = cache_k_l.shape[0]
    rows = N_HEADS * t_new
    scale = HEAD_DIM ** -0.5
    to_heads = lambda t: t.reshape(dbs, t_new, N_HEADS, HEAD_DIM).transpose(0, 2, 1, 3)
    q4, k4, v4 = to_heads(q_s * scale), to_heads(k_s), to_heads(v_s)
    ck_t = jnp.swapaxes(cache_k_l, 2, 3)
    cv_rows = jnp.swapaxes(cache_v_l, 2, 3).reshape(n_pool * N_HEADS, HEAD_DIM, PAGE_SIZE)
    scores = _score_sample(page_table, ck_t, q4.astype(BF16))
    new_rows = lambda t4: jnp.broadcast_to(t4.transpose(0, 2, 1, 3)[:, :, :, None, :],
                                           (dbs, t_new, N_HEADS, t_new, HEAD_DIM)).reshape(dbs, t_new, rows, HEAD_DIM)
    tq = jnp.broadcast_to((jnp.arange(rows, dtype=I32) % t_new).astype(F32)[:, None], (rows, LANES))
    probs, o_new, stats = _select_sample(scores, q4.reshape(dbs, rows, HEAD_DIM), new_rows(k4), new_rows(v4), tq,
                                         n_blk, t_new)
    sel = stats[:, :, :MOBA_TOPK].astype(I32)
    logical = jnp.minimum(sel[..., None] * ppb + jnp.arange(ppb, dtype=I32), n_pages - 1)
    out = _pv_sample(page_table, logical.reshape(-1), cv_rows, probs, o_new, stats, t_new, MOBA_TOPK * ppb)
    return out.reshape(dbs, N_HEADS, t_new, HEAD_DIM).transpose(0, 2, 1, 3).reshape(dbs * t_new, ATTN_WIDTH), k4, v4


def _out_group(attn_ref, conv_ref, x_ref, g1_ref, sh2_ref, sc2_ref, g2_ref, n2_ref, wo_ref, wr_hi_ref, wr_lo_ref,
               wgu_ref, wd_ref, base_ref, hp_ref, logit_ref):
    a = jnp.concatenate([attn_ref[...], conv_ref[...]], axis=1)
    x1 = x_ref[...] + g1_ref[...] * _dot(a, wo_ref[...])
    h = x1 * lax.rsqrt(jnp.mean(x1 * x1, axis=-1, keepdims=True) + EPS) * n2_ref[...]
    h = h * (1.0 + sc2_ref[...]) + sh2_ref[...]
    h_hi, h_lo = _split(h)
    logit_ref[...] = (_dot_nt(wr_hi_ref[...], h_hi) + _dot_nt(wr_lo_ref[...], h_hi) + _dot_nt(wr_hi_ref[...], h_lo))
    gu = _dot(h_hi, wgu_ref[...])
    gate = gu[:, :EXPERT_HIDDEN]
    act = gate * jax.nn.sigmoid(gate) * gu[:, EXPERT_HIDDEN:]
    base_ref[...] = x1 + g2_ref[...] * _dot(act.astype(BF16), wd_ref[...])
    _store_rows(hp_ref, h)


def _out_kernel(*refs, n_first):
    first, second, shared, outs = refs[:7], refs[7:14], refs[14:20], refs[20:]
    i = pl.program_id(0)

    @pl.when(i < n_first)
    def _():
        _out_group(*first, *shared, *outs)

    @pl.when(i >= n_first)
    def _():
        _out_group(*second, *shared, *outs)


def _out_proj(attn_p, conv_p, x_p, mod_p, n_seq_p, seq, attn_s, conv_s, x_s, mod_s, n2g, wo, wr_hi, wr_lo, wgu, wd):
    n_p, n_s = x_p.shape[0], x_s.shape[0]
    tp, ts = n_p // TOK_TILE, n_s // TOK_TILE
    n_all = n_p + n_s
    p_spec = lambda w: pl.BlockSpec((TOK_TILE, w), lambda i: (jnp.minimum(i, tp - 1), 0))
    s_spec = lambda w: pl.BlockSpec((TOK_TILE, w), lambda i: (jnp.maximum(i - tp, 0), 0))
    out_spec = lambda w: pl.BlockSpec((TOK_TILE, w), lambda i: (i, 0))
    const = lambda r, c: pl.BlockSpec((r, c), lambda i: (0, 0))
    in_specs = ([p_spec(ATTN_WIDTH), p_spec(CONV_WIDTH), p_spec(D_MODEL)]
                + [_seq_mod_spec(c, TOK_TILE, seq, n_seq_p) for c in (2, 3, 4, 5)]
                + [s_spec(ATTN_WIDTH), s_spec(CONV_WIDTH), s_spec(D_MODEL)]
                + [_tok_mod_spec(c, TOK_TILE, ts, tp) for c in (2, 3, 4, 5)]
                + [const(1, D_MODEL), const(D_MODEL, D_MODEL), const(N_EXPERTS, D_MODEL), const(N_EXPERTS, D_MODEL),
                   const(D_MODEL, 2 * EXPERT_HIDDEN), const(EXPERT_HIDDEN, D_MODEL)])
    return pl.pallas_call(
        functools.partial(_out_kernel, n_first=tp),
        out_shape=(jax.ShapeDtypeStruct((n_all, D_MODEL), F32), jax.ShapeDtypeStruct((n_all, ROW_PARTS, LANES), F32),
                   jax.ShapeDtypeStruct((N_EXPERTS, n_all), F32)),
        grid=(tp + ts,),
        in_specs=in_specs,
        out_specs=(out_spec(D_MODEL), pl.BlockSpec((TOK_TILE, ROW_PARTS, LANES), lambda i: (i, 0, 0)),
                   pl.BlockSpec((N_EXPERTS, TOK_TILE), lambda i: (0, i))),
        compiler_params=_params(("arbitrary",)),
        name="out_proj",
    )(attn_p, conv_p, x_p, mod_p, mod_p, mod_p, mod_p, attn_s, conv_s, x_s, mod_s, mod_s, mod_s, mod_s,
      n2g, wo, wr_hi, wr_lo, wgu, wd)


def _first_max(x, rows, n_rows):
    m = jnp.max(x, axis=0, keepdims=True)
    return m, jnp.min(jnp.where(x == m, rows, n_rows), axis=0, keepdims=True)


def _route_t_kernel(logit_ref, bias_ref, e_ref, w_ref, cnt_ref):
    score = jax.nn.sigmoid(logit_ref[...])
    biased = score + bias_ref[...]
    n_tok = score.shape[1]
    grp_rows = lax.broadcasted_iota(I32, (GROUP_SIZE, n_tok), 0)
    groups = [biased[g * GROUP_SIZE:(g + 1) * GROUP_SIZE, :] for g in range(N_GROUPS)]
    gs = []
    for blk in groups:
        m1, i1 = _first_max(blk, grp_rows, GROUP_SIZE)
        gs.append(m1 + jnp.max(jnp.where(grp_rows == i1, NEG_INF, blk), axis=0, keepdims=True))
    kept = []
    for g in range(N_GROUPS):
        rank = jnp.zeros(gs[g].shape, F32)
        for o in range(N_GROUPS):
            if o == g:
                continue
            ahead = (gs[o] >= gs[g]) if o < g else (gs[o] > gs[g])
            rank = rank + jnp.where(ahead, 1.0, 0.0)
        kept.append(jnp.where(rank < TOPK_GROUPS, groups[g], NEG_INF))
    masked = jnp.concatenate(kept, axis=0)
    rows = lax.broadcasted_iota(I32, score.shape, 0)
    out_row = lax.broadcasted_iota(I32, e_ref.shape, 0)
    e_out = jnp.zeros(e_ref.shape, I32)
    w_out = jnp.zeros(w_ref.shape, F32)
    w_sum = jnp.zeros((1, n_tok), F32)
    hits = jnp.zeros(score.shape, F32)
    for k in range(TOP_K):
        _, idx = _first_max(masked, rows, N_EXPERTS)
        pick = rows == idx
        wk = jnp.sum(jnp.where(pick, score, 0.0), axis=0, keepdims=True)
        masked = jnp.where(pick, NEG_INF, masked)
        hits = jnp.where(pick, 1.0, hits)
        e_out = jnp.where(out_row == k, idx, e_out)
        w_out = jnp.where(out_row == k, wk, w_out)
        w_sum = w_sum + wk
    e_ref[...] = e_out
    w_ref[...] = w_out / w_sum * ROUTED_SCALE

    @pl.when(pl.program_id(0) == 0)
    def _():
        cnt_ref[...] = jnp.zeros(cnt_ref.shape, F32)

    cnt_ref[...] += functools.reduce(lambda a, b: a + b,
                                     [hits[:, c * LANES:(c + 1) * LANES] for c in range(n_tok // LANES)])


def _route_t(logits_t, bias_col):
    n_tok = logits_t.shape[1]
    k_spec = pl.BlockSpec((TOP_K, TOK_TILE), lambda i: (0, i))
    return pl.pallas_call(
        _route_t_kernel,
        out_shape=(jax.ShapeDtypeStruct((TOP_K, n_tok), I32), jax.ShapeDtypeStruct((TOP_K, n_tok), F32),
                   jax.ShapeDtypeStruct((N_EXPERTS, LANES), F32)),
        grid=(n_tok // TOK_TILE,),
        in_specs=[pl.BlockSpec((N_EXPERTS, TOK_TILE), lambda i: (0, i)),
                  pl.BlockSpec((N_EXPERTS, 1), lambda i: (0, 0))],
        out_specs=(k_spec, k_spec, pl.BlockSpec((N_EXPERTS, LANES), lambda i: (0, 0))),
        compiler_params=_params(("arbitrary",)),
        name="route",
    )(logits_t, bias_col)


def _plan_t_kernel(e_ref, start_ref, tri_ref, pos_ref, run_ref):
    @pl.when(pl.program_id(0) == 0)
    def _():
        run_ref[...] = jnp.zeros(run_ref.shape, F32)

    rows = lax.broadcasted_iota(I32, (N_EXPERTS, TOK_TILE), 0)
    picks = [rows == e_ref[k:k + 1, :] for k in range(TOP_K)]
    hits = jnp.zeros((N_EXPERTS, TOK_TILE), F32)
    for pk in picks:
        hits = jnp.where(pk, 1.0, hits)
    before = _dot(hits.astype(BF16), tri_ref[...]) + run_ref[...] + start_ref[...]
    out_row = lax.broadcasted_iota(I32, pos_ref.shape, 0)
    out = jnp.zeros(pos_ref.shape, I32)
    for k, pk in enumerate(picks):
        dest = jnp.sum(jnp.where(pk, before, 0.0), axis=0, keepdims=True)
        out = jnp.where(out_row == k, dest.astype(I32), out)
    pos_ref[...] = out
    run_ref[...] += jnp.sum(hits, axis=1, keepdims=True)


def _plan_t(top_e_t, group_start_col):
    n_tok = top_e_t.shape[1]
    r = jnp.arange(TOK_TILE, dtype=I32)
    tri = (r[:, None] < r[None, :]).astype(BF16)
    k_spec = pl.BlockSpec((TOP_K, TOK_TILE), lambda i: (0, i))
    return pl.pallas_call(
        _plan_t_kernel,
        out_shape=jax.ShapeDtypeStruct((TOP_K, n_tok), I32),
        grid=(n_tok // TOK_TILE,),
        in_specs=[k_spec, pl.BlockSpec((N_EXPERTS, 1), lambda i: (0, 0)),
                  pl.BlockSpec((TOK_TILE, TOK_TILE), lambda i: (0, 0))],
        out_specs=k_spec,
        scratch_shapes=[pltpu.VMEM((N_EXPERTS, 1), F32)],
        compiler_params=_params(("arbitrary",)),
        name="moe_plan",
    )(top_e_t, group_start_col, tri)


def _row_copy(src_ref, src_row, dst_ref, dst_row, sem):
    return pltpu.make_async_copy(src_ref.at[src_row], dst_ref.at[dst_row], sem)


def _load_rows(ref):
    return jnp.concatenate([ref[:, c, :] for c in range(ROW_PARTS)], axis=1)


def _store_rows(ref, val):
    for c in range(ROW_PARTS):
        ref[:, c, :] = val[:, c * LANES:(c + 1) * LANES]


def _scatter_kernel(pos_ref, h_ref, init_ref, o_ref, stage, sem):
    del init_ref
    i = pl.program_id(0)
    slot = i % 2
    stage[slot] = h_ref[...]
    v = pos_ref[...]
    for t in range(ROW_TILE):
        for k in range(TOP_K):
            _row_copy(stage.at[slot], t, o_ref, v[0, t * TOP_K + k], sem.at[slot]).start()

    def drain(slot_):
        for _ in range(ROW_TILE * TOP_K):
            _row_copy(stage.at[slot_], 0, o_ref, 0, sem.at[slot_]).wait()

    @pl.when(i > 0)
    def _():
        drain(1 - slot)

    @pl.when(i == pl.num_programs(0) - 1)
    def _():
        drain(slot)


def _scatter_rows(pos3, h_packed, n_rows):
    n_tok = h_packed.shape[0]
    width = ROW_TILE * TOP_K
    zeros = jnp.zeros((n_rows, ROW_PARTS, LANES), F32)
    return pl.pallas_call(
        _scatter_kernel,
        out_shape=jax.ShapeDtypeStruct((n_rows, ROW_PARTS, LANES), F32),
        grid=(n_tok // ROW_TILE,),
        in_specs=[pl.BlockSpec((None, 1, width), lambda i: (i, 0, 0)),
                  pl.BlockSpec((ROW_TILE, ROW_PARTS, LANES), lambda i: (i, 0, 0)),
                  pl.BlockSpec(memory_space=pl.ANY)],
        out_specs=pl.BlockSpec(memory_space=pl.ANY),
        scratch_shapes=[pltpu.VMEM((2, ROW_TILE, ROW_PARTS, LANES), F32), pltpu.SemaphoreType.DMA((2,))],
        input_output_aliases={2: 0},
        compiler_params=_params(("arbitrary",)),
        name="moe_scatter",
    )(pos3, h_packed, zeros)


def _ffn_kernel(be_ref, first_ref, next_ref, slot_ref, nused_ref, x_ref, wg_hbm, wu_hbm, wd_hbm, o_ref,
                wg_buf, wu_buf, wd_buf, sem):
    i = pl.program_id(0)
    live = i < nused_ref[0]
    slot = slot_ref[i]

    def weight_copies(expert, slot_):
        return [pltpu.make_async_copy(hbm.at[expert], buf.at[slot_], sem.at[slot_, n])
                for n, (hbm, buf) in enumerate(((wg_hbm, wg_buf), (wu_hbm, wu_buf), (wd_hbm, wd_buf)))]

    @pl.when(i == 0)
    def _():
        for c in weight_copies(be_ref[0], 0):
            c.start()

    @pl.when(first_ref[i] == 1)
    def _():
        for c in weight_copies(be_ref[i], slot):
            c.wait()

        @pl.when(next_ref[i] >= 0)
        def _():
            for c in weight_copies(next_ref[i], 1 - slot):
                c.start()

    @pl.when(live)
    def _():
        x = _load_rows(x_ref).astype(BF16)
        gate = _dot(x, wg_buf[slot].astype(BF16))
        act = (gate * jax.nn.sigmoid(gate) * _dot(x, wu_buf[slot].astype(BF16))).astype(BF16)
        _store_rows(o_ref, _dot(act, wd_buf[slot].astype(BF16)))

    @pl.when(jnp.logical_not(live))
    def _():
        o_ref[...] = jnp.zeros(o_ref.shape, F32)


def _expert_runs(blk_expert, n_used):
    n_blocks = blk_expert.shape[0]
    idx = jnp.arange(n_blocks, dtype=I32)
    prev = jnp.concatenate([jnp.full((1,), -1, I32), blk_expert[:-1]])
    first = ((blk_expert != prev) & (idx < n_used[0])).astype(I32)
    slot = (jnp.cumsum(first) - 1) % 2
    start_at = jnp.where(first == 1, idx, n_blocks)
    nxt_start = lax.cummin(jnp.concatenate([start_at[1:], jnp.full((1,), n_blocks, I32)]), reverse=True)
    nxt = jnp.where(nxt_start < n_blocks, blk_expert[jnp.minimum(nxt_start, n_blocks - 1)], -1)
    return first, nxt.astype(I32), slot.astype(I32)


def _expert_ffn(blk_expert, n_used, xs, w_gate, w_up, w_down):
    n_rows = xs.shape[0]
    n_blocks = n_rows // MOE_BLOCK
    first, nxt, slot = _expert_runs(blk_expert, n_used)
    row_spec = pl.BlockSpec((MOE_BLOCK, ROW_PARTS, LANES), lambda i, *_: (i, 0, 0))
    hbm = pl.BlockSpec(memory_space=pl.ANY)
    grid_spec = pltpu.PrefetchScalarGridSpec(
        num_scalar_prefetch=5,
        grid=(n_blocks,),
        in_specs=[row_spec, hbm, hbm, hbm],
        out_specs=row_spec,
        scratch_shapes=[pltpu.VMEM((2, D_MODEL, EXPERT_HIDDEN), F32), pltpu.VMEM((2, D_MODEL, EXPERT_HIDDEN), F32),
                        pltpu.VMEM((2, EXPERT_HIDDEN, D_MODEL), F32), pltpu.SemaphoreType.DMA((2, 3))])
    return pl.pallas_call(
        _ffn_kernel,
        out_shape=jax.ShapeDtypeStruct((n_rows, ROW_PARTS, LANES), F32),
        grid_spec=grid_spec,
        compiler_params=_params(("arbitrary",)),
        name="moe_ffn",
    )(blk_expert, first, nxt, slot, n_used, xs, w_gate, w_up, w_down)


def _combine_kernel(cur_ref, nxt_ref, y_ref, w_ref, base_ref, g2_ref, o_ref, ybuf, sem):
    i = pl.program_id(0)
    slot = i % 2

    def start(pos_ref, slot_):
        v = pos_ref[...]
        for t in range(ROW_TILE):
            for k in range(TOP_K):
                _row_copy(y_ref, v[0, t * TOP_K + k], ybuf.at[slot_, k], t, sem.at[slot_]).start()

    @pl.when(i == 0)
    def _():
        start(cur_ref, slot)

    @pl.when(i + 1 < pl.num_programs(0))
    def _():
        start(nxt_ref, 1 - slot)

    for _ in range(ROW_TILE * TOP_K):
        _row_copy(y_ref, 0, ybuf.at[slot, 0], 0, sem.at[slot]).wait()
    for c in range(ROW_PARTS):
        cols = slice(c * LANES, (c + 1) * LANES)
        acc = jnp.zeros((ROW_TILE, LANES), F32)
        for k in range(TOP_K):
            acc = acc + w_ref[:, k:k + 1] * ybuf[slot, k, :, c, :]
        o_ref[:, cols] = base_ref[:, cols] + g2_ref[:, cols] * acc


def _combine(pos3, y_rows, top_w, base, g2_spec, mod, n_tok, tile_off):
    n_steps = n_tok // ROW_TILE
    last = tile_off + n_steps - 1
    width = ROW_TILE * TOP_K
    return pl.pallas_call(
        _combine_kernel,
        out_shape=jax.ShapeDtypeStruct((n_tok, D_MODEL), F32),
        grid=(n_steps,),
        in_specs=[pl.BlockSpec((None, 1, width), lambda i: (i + tile_off, 0, 0)),
                  pl.BlockSpec((None, 1, width), lambda i: (jnp.minimum(i + tile_off + 1, last), 0, 0)),
                  pl.BlockSpec(memory_space=pl.ANY),
                  pl.BlockSpec((ROW_TILE, TOP_K), lambda i: (i + tile_off, 0)),
                  pl.BlockSpec((ROW_TILE, D_MODEL), lambda i: (i + tile_off, 0)),
                  g2_spec],
        out_specs=pl.BlockSpec((ROW_TILE, D_MODEL), lambda i: (i, 0)),
        scratch_shapes=[pltpu.VMEM((2, TOP_K, ROW_TILE, ROW_PARTS, LANES), F32), pltpu.SemaphoreType.DMA((2,))],
        compiler_params=_params(("arbitrary",)),
        name="moe_combine",
    )(pos3, pos3, y_rows, top_w, base, mod)


def _block_table(counts):
    padded = (counts.astype(I32) + MOE_BLOCK - 1) // MOE_BLOCK * MOE_BLOCK
    return padded, jnp.cumsum(padded, axis=1)


def kernel(x_prompt, x_sample, c_prompt, c_sample, cache_k, cache_v, state_conv, page_table, norm1_g, norm2_g, w_ada, b_ada, w_in, q_norm_g, k_norm_g, conv_w, conv_b, conv_ln_g, conv_ln_b, w_out, w_router, router_bias, w_gate, w_up, w_down, w_shared_gate, w_shared_up, w_shared_down):
    depth = w_in.shape[0]
    assert depth == 1, "single-layer step only"
    bsz, seq, _ = x_prompt.shape
    dbs, t_new, _ = x_sample.shape
    n_pages = page_table.shape[1]
    ppb = MOBA_BLOCK // PAGE_SIZE
    n_p, n_s = bsz * seq, dbs * t_new
    n_all = n_p + n_s
    assert cache_k.shape[3] == PAGE_SIZE and n_pages % ppb == 0, "past length must be whole MoBA blocks"
    assert n_pages % SCORE_CHUNK == 0 and MOBA_TOPK <= n_pages // ppb <= LANES
    assert seq % TOK_TILE == 0 and n_s % TOK_TILE == 0 and seq // MOBA_BLOCK > MOBA_TOPK
    assert n_p % ROW_TILE == 0 and n_s % ROW_TILE == 0
    ly = 0

    n_c = bsz + dbs
    n_c_pad = -(-n_c // 8) * 8
    c_all = jnp.concatenate([c_prompt, c_sample, jnp.zeros((n_c_pad - n_c, D_MODEL), F32)], axis=0)
    mod = _adaln(c_all, w_ada[ly], b_ada[ly][None, :])
    mod_p = mod[:bsz].reshape(bsz, 1, 6 * D_MODEL)
    mod_s = jnp.repeat(mod[bsz:n_c], t_new, axis=0)

    w_in_bf = w_in[ly].astype(BF16)
    qg = jnp.tile(q_norm_g[ly], N_HEADS)[None, :]
    kg = jnp.tile(k_norm_g[ly], N_HEADS)[None, :]
    head_id = jnp.arange(ATTN_WIDTH, dtype=I32) // HEAD_DIM
    head_mean = jnp.where(head_id[:, None] == head_id[None, :], 1.0 / HEAD_DIM, 0.0).astype(BF16)
    n1g = norm1_g[ly][None, :]
    n2g = norm2_g[ly][None, :]
    cw, cb = conv_w[ly], conv_b[ly][None, :]
    ln_g, ln_b = conv_ln_g[ly][None, :], conv_ln_b[ly][None, :]

    xp2 = x_prompt.reshape(n_p, D_MODEL)
    q_p, k_p, vt_p, k_pages, v_pages, k_mean, u_p = _project(xp2, mod_p, bsz, seq, n1g, w_in_bf, qg, kg,
                                                             head_mean, True)
    shp = (bsz, seq, ATTN_WIDTH)
    attn_p = _attn_prompt(q_p.reshape(shp), k_p.reshape(shp), vt_p,
                          k_mean.reshape(bsz, seq // MOBA_BLOCK, ATTN_WIDTH))
    u_p3 = u_p.reshape(bsz, seq, CONV_WIDTH)
    conv_p = _conv_prompt(u_p3, cw, cb, ln_g, ln_b)
    page_shape = (1, bsz, seq // PAGE_SIZE, N_HEADS, PAGE_SIZE, HEAD_DIM)
    k_prompt = jnp.swapaxes(k_pages, 2, 3).reshape(page_shape)
    v_prompt = jnp.swapaxes(v_pages, 2, 3).reshape(page_shape)
    conv_prompt = u_p3[:, seq - (CONV_KERNEL - 1):, :][None]

    xs2 = x_sample.reshape(n_s, D_MODEL)
    q_s, k_s, v_s, u_s = _project(xs2, mod_s, dbs, t_new, n1g, w_in_bf, qg, kg, head_mean, False)
    attn_s, k4, v4 = _moba_sample(q_s, k_s, v_s, cache_k[ly], cache_v[ly], page_table)
    u_s3 = u_s.reshape(dbs, t_new, CONV_WIDTH)
    u_ext = jnp.concatenate([state_conv[ly], u_s3], axis=1)
    conv_s = _conv_sample(u_ext.transpose(1, 0, 2), cw, cb, ln_g, ln_b, t_new)
    conv_s = conv_s.transpose(1, 0, 2).reshape(n_s, CONV_WIDTH)
    k_sample, v_sample = k4[None], v4[None]
    conv_sample = u_ext[:, t_new:, :][None]

    wo = w_out[ly].astype(BF16)
    wr_hi, wr_lo = _split(w_router[ly].T)
    wgu = jnp.concatenate([w_shared_gate[ly], w_shared_up[ly]], axis=1).astype(BF16)
    wd = w_shared_down[ly].astype(BF16)
    base, h_packed, logits_t = _out_proj(attn_p.reshape(n_p, ATTN_WIDTH), conv_p.reshape(n_p, CONV_WIDTH), xp2,
                                         mod_p, bsz, seq, attn_s.astype(BF16), conv_s, xs2, mod_s, n2g, wo, wr_hi,
                                         wr_lo, wgu, wd)

    top_e_t, top_w_t, lane_counts = _route_t(logits_t, router_bias[ly][:, None])
    n_assign = n_all * TOP_K
    n_blocks = -(-(n_assign + N_EXPERTS * (MOE_BLOCK - 1)) // MOE_BLOCK)
    n_rows = n_blocks * MOE_BLOCK
    padded, pad_end = _block_table(jnp.sum(lane_counts, axis=1)[None, :])
    group_start = (pad_end - padded).astype(F32).reshape(N_EXPERTS, 1)
    blk_first_row = jnp.arange(n_blocks, dtype=I32)[:, None] * MOE_BLOCK
    blk_expert = jnp.minimum(jnp.sum((pad_end <= blk_first_row).astype(I32), axis=1), N_EXPERTS - 1)
    n_used = (pad_end[0, -1:] // MOE_BLOCK).astype(I32)
    pos_t = _plan_t(top_e_t, group_start)
    pos3 = pos_t.T.reshape(n_all // ROW_TILE, 1, ROW_TILE * TOP_K)
    top_w = top_w_t.T
    xs_rows = _scatter_rows(pos3, h_packed, n_rows)
    y_rows = _expert_ffn(blk_expert, n_used, xs_rows, w_gate[ly], w_up[ly], w_down[ly])
    g2_p = _seq_mod_spec(5, ROW_TILE, seq, bsz)
    g2_s = _tok_mod_spec(5, ROW_TILE, n_s // ROW_TILE)
    y_p = _combine(pos3, y_rows, top_w, base, g2_p, mod_p, n_p, 0)
    y_s = _combine(pos3, y_rows, top_w, base, g2_s, mod_s, n_s, n_p // ROW_TILE)

    return (y_p.reshape(bsz, seq, D_MODEL), y_s.reshape(dbs, t_new, D_MODEL), k_prompt, v_prompt, conv_prompt,
            k_sample, v_sample, conv_sample)
```
